```python
import math
import jax, jax.numpy as jnp
from jax import lax
import numpy as np

D_MODEL = 1024
BATCH = 32
SEQ = 2048
DEPTH = 4

GRID_W = 64
CTX_LEN = 256
N_MIXERS = 3
N_MOD = 9
FFN_HIDDEN = 256 * ((8 * D_MODEL + 767) // 768)
FFN_RESIDUAL = 0.5
DN_HEAD_DIM = 128
DN_HEADS = D_MODEL // DN_HEAD_DIM
DN_CONV = 5
DN_CHUNK = 64
SWA_HEAD_DIM = 64
SWA_HEADS = D_MODEL // SWA_HEAD_DIM
SWA_KV_HEADS = SWA_HEADS // 4
WINDOW = 128
ATTN_BLOCK = 128
RET_HEADS = 4
RET_QK_DIM = D_MODEL // RET_HEADS
RET_V_DIM = 2 * RET_QK_DIM
RET_CHUNK = 128
ROPE_BASE = 10000.0
EPS = 1e-6

kernel_name = "hybrid_deltanet_swa_retention_dit"


def _rms_norm(x, gain):
    xf = x.astype(jnp.float32)
    y = xf * lax.rsqrt(jnp.mean(xf * xf, axis=-1, keepdims=True) + EPS)
    return (y * gain.astype(jnp.float32)).astype(x.dtype)


def _head_group_norm(x, gain):
    xf = x.astype(jnp.float32)
    mu = jnp.mean(xf, axis=-1, keepdims=True)
    var = jnp.mean(jnp.square(xf - mu), axis=-1, keepdims=True)
    return ((xf - mu) * lax.rsqrt(var + EPS) * gain.astype(jnp.float32)).astype(x.dtype)


def _l2_normalize(x):
    xf = x.astype(jnp.float32)
    return (xf * lax.rsqrt(jnp.sum(xf * xf, axis=-1, keepdims=True) + EPS)).astype(x.dtype)


def _modulated_norm(h, gain, shift, scale):
    return _rms_norm(h, gain) * (1.0 + scale) + shift


def _swiglu(h, w1, w2):
    a, b = jnp.split(h @ w1, 2, axis=-1)
    return (jax.nn.silu(a) * b) @ w2


def _rotary_tables(positions, n_freq):
    inv = ROPE_BASE ** (-jnp.arange(n_freq, dtype=jnp.float32) / n_freq)
    ang = jnp.concatenate([p.astype(jnp.float32)[:, None] * inv[None, :] for p in positions], axis=-1)
    return jnp.cos(ang), jnp.sin(ang)


def _rope(x, cos, sin):
    x1, x2 = jnp.split(x, 2, axis=-1)
    c = cos[:, None, :].astype(x.dtype)
    s = sin[:, None, :].astype(x.dtype)
    return jnp.concatenate([x1 * c - x2 * s, x1 * s + x2 * c], axis=-1)


def _short_conv(x, w):
    k = w.shape[0]
    return lax.conv_general_dilated(x, w[:, None, :], window_strides=(1,), padding=[(k // 2, k // 2)],
                                    dimension_numbers=('NWC', 'WIO', 'NWC'), feature_group_count=x.shape[-1])


def _join(a_ctx, a_lat, reverse):
    if reverse:
        a_ctx, a_lat = jnp.flip(a_ctx, 1), jnp.flip(a_lat, 1)
    return jnp.concatenate([a_ctx, a_lat], axis=1)


def _both_dirs(a_ctx, a_lat):
    return jnp.concatenate([_join(a_ctx, a_lat, False), _join(a_ctx, a_lat, True)], axis=0)


def _split_dirs(o, n_batch, ctx_len):
    of, ob = o[:n_batch], o[n_batch:]
    return (of[:, :ctx_len], of[:, ctx_len:], jnp.flip(ob[:, :ctx_len], 1), jnp.flip(ob[:, ctx_len:], 1))


def _to_chunks(a, size):
    n, t, h = a.shape[:3]
    a = a.astype(jnp.float32).reshape((n, t // size, size, h) + a.shape[3:])
    return jnp.moveaxis(a, 3, 1)


def _from_chunks(a):
    n, h, nc, c = a.shape[:4]
    return jnp.moveaxis(a, 1, 3).reshape((n, nc * c, h) + a.shape[4:])


def _gated_delta_chunked(q, k, v, g, beta):
    out_dtype = v.dtype
    n, t, h, dk = q.shape
    dv = v.shape[-1]
    q = _to_chunks(q, DN_CHUNK) * dk ** -0.5
    k = _to_chunks(k, DN_CHUNK)
    v = _to_chunks(v, DN_CHUNK)
    gc = jnp.cumsum(_to_chunks(g, DN_CHUNK), axis=-1)
    beta = _to_chunks(beta, DN_CHUNK)
    causal = jnp.tril(jnp.ones((DN_CHUNK, DN_CHUNK), bool))
    strict = jnp.tril(jnp.ones((DN_CHUNK, DN_CHUNK), bool), -1)
    decay = jnp.exp(jnp.where(causal, gc[..., :, None] - gc[..., None, :], -jnp.inf))
    kb = k * beta[..., None]
    lower = jnp.where(strict, jnp.einsum('nhcid,nhcjd->nhcij', kb, k) * decay, 0.0)
    rhs = jnp.concatenate([v * beta[..., None], kb * jnp.exp(gc)[..., None]], axis=-1)
    uw = lax.linalg.triangular_solve(lower, rhs, left_side=True, lower=True, unit_diagonal=True)
    u, w = uw[..., :dv], uw[..., dv:]
    qk = jnp.einsum('nhcid,nhcjd->nhcij', q, k) * decay
    qg = q * jnp.exp(gc)[..., None]
    g_last = gc[..., -1]
    kd = k * jnp.exp(g_last[..., None] - gc)[..., None]
    xs = tuple(jnp.moveaxis(a, 2, 0) for a in (u, w, qk, qg, kd, jnp.exp(g_last)))

    def step(state, inp):
        u_c, w_c, qk_c, qg_c, kd_c, a_c = inp
        v_new = u_c - jnp.einsum('nhik,nhkv->nhiv', w_c, state)
        o = jnp.einsum('nhik,nhkv->nhiv', qg_c, state) + jnp.einsum('nhij,nhjv->nhiv', qk_c, v_new)
        state = state * a_c[..., None, None] + jnp.einsum('nhjk,nhjv->nhkv', kd_c, v_new)
        return state, o

    _, o = lax.scan(step, jnp.zeros((n, h, dk, dv), jnp.float32), xs)
    return _from_chunks(jnp.moveaxis(o, 0, 2)).astype(out_dtype)


def _retention_log_decay():
    return jnp.log1p(-jnp.exp2(-5.0 - jnp.arange(RET_HEADS, dtype=jnp.float32)))


def _retention_chunked(q, k, v, log_gamma):
    out_dtype = v.dtype
    n, t, h, dk = q.shape
    dv = v.shape[-1]
    q = _to_chunks(q, RET_CHUNK)
    k = _to_chunks(k, RET_CHUNK) * dk ** -0.5
    v = _to_chunks(v, RET_CHUNK)
    pos = jnp.arange(RET_CHUNK, dtype=jnp.float32)
    lg = log_gamma[:, None]
    rel = pos[:, None] - pos[None, :]
    decay = jnp.exp(jnp.where(rel >= 0, lg[:, :, None] * rel, -jnp.inf))
    scores = jnp.einsum('nhcid,nhcjd->nhcij', q, k) * decay[:, None]
    o_intra = jnp.einsum('nhcij,nhcje->nhcie', scores, v)
    q_in = q * jnp.exp(lg * (pos + 1.0))[:, None, :, None]
    k_st = k * jnp.exp(lg * (RET_CHUNK - 1.0 - pos))[:, None, :, None]
    chunk_decay = jnp.exp(lg * RET_CHUNK)[:, :, None]

    def step(state, inp):
        q_c, k_c, v_c = inp
        o = jnp.einsum('nhid,nhde->nhie', q_c, state)
        state = state * chunk_decay + jnp.einsum('nhjd,nhje->nhde', k_c, v_c)
        return state, o

    xs = tuple(jnp.moveaxis(a, 2, 0) for a in (q_in, k_st, v))
    _, o_inter = lax.scan(step, jnp.zeros((n, h, dk, dv), jnp.float32), xs)
    return _from_chunks(o_intra + jnp.moveaxis(o_inter, 0, 2)).astype(out_dtype)


def _sink_softmax(logits, sink):
    sink_col = jnp.broadcast_to(sink.astype(jnp.float32)[None, :, :, None, None], logits.shape[:-1] + (1,))
    return jax.nn.softmax(jnp.concatenate([sink_col, logits], axis=-1), axis=-1)[..., 1:]


def _deltanet_mixer(hc, hx, w_in, conv_w, a_log, dt_bias, o_gain, w_out):
    n_batch, ctx_len = hx.shape[0], hc.shape[1]
    qkv_w = 3 * DN_HEADS * DN_HEAD_DIM
    z_w = DN_HEADS * DN_HEAD_DIM

    def project(h):
        n = h.shape[1]
        qkv, z, ab = jnp.split(h @ w_in, [qkv_w, qkv_w + z_w], axis=-1)
        qkv = jax.nn.silu(_short_conv(qkv, conv_w))
        q, k, v = [a.reshape(n_batch, n, DN_HEADS, DN_HEAD_DIM) for a in jnp.split(qkv, 3, axis=-1)]
        ab = ab.astype(jnp.float32).reshape(n_batch, n, 2, 2, DN_HEADS)
        g = -jnp.exp(a_log.astype(jnp.float32)) * jax.nn.softplus(ab[:, :, :, 0] + dt_bias.astype(jnp.float32))
        beta = jax.nn.sigmoid(ab[:, :, :, 1])
        return _l2_normalize(q), _l2_normalize(k), v, z.reshape(n_batch, n, DN_HEADS, DN_HEAD_DIM), g, beta

    qc, kc, vc, zc, gc, bc = project(hc)
    qx, kx, vx, zx, gx, bx = project(hx)
    g_all = jnp.concatenate([_join(gc[:, :, 0], gx[:, :, 0], False), _join(gc[:, :, 1], gx[:, :, 1], True)], axis=0)
    b_all = jnp.concatenate([_join(bc[:, :, 0], bx[:, :, 0], False), _join(bc[:, :, 1], bx[:, :, 1], True)], axis=0)
    o = _gated_delta_chunked(_both_dirs(qc, qx), _both_dirs(kc, kx), _both_dirs(vc, vx), g_all, b_all)
    ofc, ofx, obc, obx = _split_dirs(o, n_batch, ctx_len)

    def finish(o_f, o_b, z):
        y = _rms_norm(o_f + o_b, o_gain) * jax.nn.silu(z)
        return y.reshape(y.shape[0], y.shape[1], -1) @ w_out

    return finish(ofc, obc, zc), finish(ofx, obx, zx)


def _window_attention_mixer(hc, hx, w_qkv, q_gain, k_gain, sink, w_out, cos, sin):
    n_batch, n_tok = hx.shape[0], hx.shape[1]
    ctx_len = hc.shape[1]
    group = SWA_HEADS // SWA_KV_HEADS
    q_w = SWA_HEADS * SWA_HEAD_DIM
    kv_w = SWA_KV_HEADS * SWA_HEAD_DIM
    scale = SWA_HEAD_DIM ** -0.5
    sink_hg = sink.reshape(SWA_KV_HEADS, group)

    def project(h, rotate):
        n = h.shape[1]
        q, k, v = jnp.split(h @ w_qkv, [q_w, q_w + kv_w], axis=-1)
        q = _rms_norm(q.reshape(n_batch, n, SWA_HEADS, SWA_HEAD_DIM), q_gain)
        k = _rms_norm(k.reshape(n_batch, n, SWA_KV_HEADS, SWA_HEAD_DIM), k_gain)
        if rotate:
            q, k = _rope(q, cos, sin), _rope(k, cos, sin)
        q = (q * scale).reshape(n_batch, n, SWA_KV_HEADS, group, SWA_HEAD_DIM)
        return q, k, v.reshape(n_batch, n, SWA_KV_HEADS, SWA_HEAD_DIM)

    qc, kc, vc = project(hc, False)
    qx, kx, vx = project(hx, True)

    s_c = jnp.einsum('bqhgd,bkhd->bhgqk', qc, kc).astype(jnp.float32)
    p_c = _sink_softmax(s_c, sink_hg).astype(vc.dtype)
    oc = jnp.einsum('bhgqk,bkhd->bqhgd', p_c, vc).reshape(n_batch, ctx_len, q_w)

    span = ATTN_BLOCK + 2 * WINDOW
    pad = ((0, 0), (WINDOW, WINDOW), (0, 0), (0, 0))
    kp, vp = jnp.pad(kx, pad), jnp.pad(vx, pad)
    q_off = jnp.arange(ATTN_BLOCK)
    k_off = jnp.arange(span)

    def block(j):
        start = j * ATTN_BLOCK
        qb = lax.dynamic_slice_in_dim(qx, start, ATTN_BLOCK, axis=1)
        kb = lax.dynamic_slice_in_dim(kp, start, span, axis=1)
        vb = lax.dynamic_slice_in_dim(vp, start, span, axis=1)
        t_pos = start + q_off
        s_pos = start - WINDOW + k_off
        allowed = (jnp.abs(t_pos[:, None] - s_pos[None, :]) <= WINDOW) & (s_pos >= 0) & (s_pos < n_tok)
        s_ctx = jnp.einsum('bqhgd,bkhd->bhgqk', qb, kc).astype(jnp.float32)
        s_win = jnp.where(allowed, jnp.einsum('bqhgd,bkhd->bhgqk', qb, kb).astype(jnp.float32), -jnp.inf)
        p = _sink_softmax(jnp.concatenate([s_ctx, s_win], axis=-1), sink_hg).astype(vb.dtype)
        return (jnp.einsum('bhgqk,bkhd->bqhgd', p[..., :ctx_len], vc)
                + jnp.einsum('bhgqk,bkhd->bqhgd', p[..., ctx_len:], vb))

    ox = lax.map(block, jnp.arange(n_tok // ATTN_BLOCK))
    ox = jnp.moveaxis(ox, 0, 1).reshape(n_batch, n_tok, q_w)
    return oc @ w_out, ox @ w_out


def _retention_mixer(hc, hx, w_in, gn_gain, w_out, cos, sin):
    n_batch, ctx_len = hx.shape[0], hc.shape[1]
    qk_w = RET_HEADS * RET_QK_DIM
    v_w = RET_HEADS * RET_V_DIM

    def project(h):
        n = h.shape[1]
        q, k, v, gf, gb = jnp.split(h @ w_in, [qk_w, 2 * qk_w, 2 * qk_w + v_w, 2 * qk_w + 2 * v_w], axis=-1)
        hq = lambda a: a.reshape(n_batch, n, RET_HEADS, RET_QK_DIM)
        hv = lambda a: a.reshape(n_batch, n, RET_HEADS, RET_V_DIM)
        return hq(q), hq(k), hv(v), hv(gf), hv(gb)

    qc, kc, vc, gfc, gbc = project(hc)
    qx, kx, vx, gfx, gbx = project(hx)
    qx, kx = _rope(qx, cos, sin), _rope(kx, cos, sin)
    o = _retention_chunked(_both_dirs(qc, qx), _both_dirs(kc, kx), _both_dirs(vc, vx), _retention_log_decay())
    ofc, ofx, obc, obx = _split_dirs(o, n_batch, ctx_len)
    gain = gn_gain.reshape(2, RET_HEADS, RET_V_DIM)

    def finish(o_f, o_b, g_f, g_b):
        y = (jax.nn.silu(g_f) * _head_group_norm(o_f, gain[0]) + jax.nn.silu(g_b) * _head_group_norm(o_b, gain[1]))
        return y.reshape(y.shape[0], y.shape[1], -1) @ w_out

    return finish(ofc, obc, gfc, gbc), finish(ofx, obx, gfx, gbx)


def setup_inputs(seed: int = 0) -> dict:
    key = jax.random.key(seed)
    ks = jax.random.split(key, 24)
    d, f = D_MODEL, FFN_HIDDEN
    n_dn, n_swa, n_ret = (len(range(kind, DEPTH, N_MIXERS)) for kind in range(N_MIXERS))

    def nrm(k, shape, std):
        return std * jax.random.normal(k, shape, jnp.float32)

    dn_in = 4 * DN_HEADS * DN_HEAD_DIM + 4 * DN_HEADS
    swa_in = (SWA_HEADS + 2 * SWA_KV_HEADS) * SWA_HEAD_DIM
    ret_v = RET_HEADS * RET_V_DIM
    ret_in = 2 * RET_HEADS * RET_QK_DIM + 3 * ret_v
    dt = jnp.exp(jax.random.uniform(ks[12], (n_dn, 2, DN_HEADS), jnp.float32, math.log(1e-3), math.log(1e-1)))
    return {
        "x": nrm(ks[0], (BATCH, SEQ, d), 1.0),
        "c": nrm(ks[1], (BATCH, d), 1.0),
        "ctx": nrm(ks[2], (BATCH, CTX_LEN, d), 1.0),
        "c_ctx": nrm(ks[3], (d,), 1.0),
        "ada_w": nrm(ks[4], (DEPTH, d, N_MOD * d), 0.5 * d ** -0.5),
        "ada_b": nrm(ks[5], (DEPTH, N_MOD * d), 0.02),
        "norm_g": 1.0 + nrm(ks[6], (DEPTH, 3, d), 0.02),
        "ffn_w1": nrm(ks[7], (DEPTH, 2, d, 2 * f), d ** -0.5),
        "ffn_w2": nrm(ks[8], (DEPTH, 2, f, d), f ** -0.5),
        "dn_w_in": nrm(ks[9], (n_dn, d, dn_in), d ** -0.5),
        "dn_conv": nrm(ks[10], (n_dn, DN_CONV, 3 * DN_HEADS * DN_HEAD_DIM), DN_CONV ** -0.5),
        "dn_a_log": jnp.log(jax.random.uniform(ks[11], (n_dn, 2, DN_HEADS), jnp.float32, 1.0, 16.0)),
        "dn_dt_bias": dt + jnp.log(-jnp.expm1(-dt)),
        "dn_o_gain": 1.0 + nrm(ks[13], (n_dn, DN_HEAD_DIM), 0.02),
        "dn_w_out": nrm(ks[14], (n_dn, DN_HEADS * DN_HEAD_DIM, d), (DN_HEADS * DN_HEAD_DIM) ** -0.5),
        "swa_w_qkv": nrm(ks[15], (n_swa, d, swa_in), d ** -0.5),
        "swa_q_gain": 1.0 + nrm(ks[16], (n_swa, SWA_HEAD_DIM), 0.02),
        "swa_k_gain": 1.0 + nrm(ks[17], (n_swa, SWA_HEAD_DIM), 0.02),
        "swa_sink": nrm(ks[18], (n_swa, SWA_HEADS), 1.0),
        "swa_w_out": nrm(ks[19], (n_swa, SWA_HEADS * SWA_HEAD_DIM, d), (SWA_HEADS * SWA_HEAD_DIM) ** -0.5),
        "ret_w_in": nrm(ks[20], (n_ret, d, ret_in), d ** -0.5),
        "ret_gn_gain": 1.0 + nrm(ks[21], (n_ret, 2, ret_v), 0.02),
        "ret_w_out": nrm(ks[22], (n_ret, ret_v, d), ret_v ** -0.5),
    }


def reference(x, c, ctx, c_ctx, ada_w, ada_b, norm_g, ffn_w1, ffn_w2, dn_w_in, dn_conv, dn_a_log, dn_dt_bias,
              dn_o_gain, dn_w_out, swa_w_qkv, swa_q_gain, swa_k_gain, swa_sink, swa_w_out, ret_w_in, ret_gn_gain,
              ret_w_out):
    n_batch, n_tok, d = x.shape
    ROWS = n_tok // GRID_W
    rows = jnp.repeat(jnp.arange(ROWS), GRID_W)
    cols = jnp.tile(jnp.arange(GRID_W), ROWS)
    swa_cos, swa_sin = _rotary_tables((rows, cols), SWA_HEAD_DIM // 4)
    ret_cos, ret_sin = _rotary_tables((jnp.arange(n_tok),), RET_QK_DIM // 2)
    cond_x = jax.nn.silu(c)
    cond_c = jax.nn.silu(c_ctx)

    for i in range(DEPTH):
        kind, slot, last = i % N_MIXERS, i // N_MIXERS, i == DEPTH - 1
        mx = (cond_x @ ada_w[i] + ada_b[i]).reshape(n_batch, 3, 3, 1, d)
        mc = (cond_c @ ada_w[i] + ada_b[i]).reshape(3, 3, d)

        x = x + FFN_RESIDUAL * mx[:, 0, 2] * _swiglu(
            _modulated_norm(x, norm_g[i, 0], mx[:, 0, 0], mx[:, 0, 1]), ffn_w1[i, 0], ffn_w2[i, 0])
        ctx = ctx + FFN_RESIDUAL * mc[0, 2] * _swiglu(
            _modulated_norm(ctx, norm_g[i, 0], mc[0, 0], mc[0, 1]), ffn_w1[i, 0], ffn_w2[i, 0])

        hx = _modulated_norm(x, norm_g[i, 1], mx[:, 1, 0], mx[:, 1, 1])
        hc = _modulated_norm(ctx, norm_g[i, 1], mc[1, 0], mc[1, 1])
        if kind == 0:
            oc, ox = _deltanet_mixer(hc, hx, dn_w_in[slot], dn_conv[slot], dn_a_log[slot], dn_dt_bias[slot],
                                     dn_o_gain[slot], dn_w_out[slot])
        elif kind == 1:
            oc, ox = _window_attention_mixer(hc, hx, swa_w_qkv[slot], swa_q_gain[slot], swa_k_gain[slot],
                                             swa_sink[slot], swa_w_out[slot], swa_cos, swa_sin)
        else:
            oc, ox = _retention_mixer(hc, hx, ret_w_in[slot], ret_gn_gain[slot], ret_w_out[slot], ret_cos, ret_sin)
        x = x + mx[:, 1, 2] * ox

        x = x + FFN_RESIDUAL * mx[:, 2, 2] * _swiglu(
            _modulated_norm(x, norm_g[i, 2], mx[:, 2, 0], mx[:, 2, 1]), ffn_w1[i, 1], ffn_w2[i, 1])
        if not last:
            ctx = ctx + mc[1, 2] * oc
            ctx = ctx + FFN_RESIDUAL * mc[2, 2] * _swiglu(
                _modulated_norm(ctx, norm_g[i, 2], mc[2, 0], mc[2, 1]), ffn_w1[i, 1], ffn_w2[i, 1])
    return x
```

```python
import functools
import math

import jax
import jax.numpy as jnp
from jax import lax
from jax.experimental import pallas as pl
from jax.experimental.pallas import tpu as pltpu

F32 = jnp.float32
BF16 = jnp.bfloat16

N_MIXERS = 3
GRID_W = 64
ROPE_BASE = 10000.0
EPS = 1e-6
FFN_RESIDUAL = 0.5
DN_HEAD_DIM = 128
DN_CHUNK = 64
DN_INV_BLOCK = 16
SWA_HEAD_DIM = 64
SWA_GROUP = 4
WINDOW = 128
ATTN_BLOCK = 128
RET_HEADS = 4
RET_CHUNK = 128

LANE = 128
SUBLANE = 8
VMEM_LIMIT_BYTES = 52 * 1024 * 1024
NEG_BIG = -1e30


def _params(*semantics):
    return pltpu.CompilerParams(dimension_semantics=semantics, vmem_limit_bytes=VMEM_LIMIT_BYTES)


def _resident(block_shape, index_map):
    return pl.BlockSpec(block_shape, index_map, pipeline_mode=pl.Buffered(1))


def _dot(a, b):
    return jnp.dot(a, b, preferred_element_type=F32)


def _dot_nt(a, b):
    return lax.dot_general(a, b, (((1,), (1,)), ((), ())), preferred_element_type=F32)


def _dot_tn(a, b):
    return lax.dot_general(a, b, (((0,), (0,)), ((), ())), preferred_element_type=F32)


def _mm(a, b):
    return _dot(a.astype(BF16), b.astype(BF16))


def _split2(x):
    hi = x.astype(BF16)
    lo = (x - hi.astype(F32)).astype(BF16)
    return hi, lo


def _split3(x):
    hi = x.astype(BF16)
    r = x - hi.astype(F32)
    mid = r.astype(BF16)
    lo = (r - mid.astype(F32)).astype(BF16)
    return hi, mid, lo


def _dot_x3(a, b):
    ah, al = _split2(a)
    bh, bl = _split2(b)
    return _dot(ah, bh) + (_dot(ah, bl) + _dot(al, bh))


def _dot_exact_lhs(m_bf16, x):
    x1, x2, x3 = _split3(x)
    return _dot(m_bf16, x1) + (_dot(m_bf16, x2) + _dot(m_bf16, x3))


def _sigmoid(x):
    return 1.0 / (1.0 + jnp.exp(-x))


def _silu(x):
    return x * _sigmoid(x)


def _softplus(x):
    return jnp.maximum(x, 0.0) + jnp.log(1.0 + jnp.exp(-jnp.abs(x)))


def _mod_norm(x, gain, shift, scale):
    ms = jnp.mean(x * x, axis=-1, keepdims=True)
    return (x * lax.rsqrt(ms + EPS)) * (gain * (1.0 + scale)) + shift


def _ada_kernel(c_ref, w_ref, b_ref, o_ref):
    o_ref[...] = _dot_x3(_silu(c_ref[...]), w_ref[...]) + b_ref[...]


def _ada_mods(cond, ada_w, ada_b):
    depth, d, nd = ada_w.shape
    bp = cond.shape[0]
    tn = d
    return pl.pallas_call(
        _ada_kernel,
        grid=(depth, nd // tn),
        in_specs=[
            pl.BlockSpec((bp, d), lambda l, n: (0, 0)),
            pl.BlockSpec((None, d, tn), lambda l, n: (l, 0, n)),
            pl.BlockSpec((None, 1, tn), lambda l, n: (l, 0, n)),
        ],
        out_specs=pl.BlockSpec((None, bp, tn), lambda l, n: (l, 0, n)),
        out_shape=jax.ShapeDtypeStruct((depth, bp, nd), F32),
        compiler_params=_params("parallel", "parallel"),
        name="ada_mods",
    )(cond, ada_w, ada_b.reshape(depth, 1, nd))


def _row_tile(lc, s):
    tm = 256
    while lc % tm or s % tm:
        tm //= 2
    return tm


def _mod_index(n_batch, n_ctx_tiles):
    return lambda b, t: (jnp.where(t < n_ctx_tiles, n_batch, b), 0, 0)


def _ffn_kernel(x_ref, mod_ref, g_ref, w1_ref, w2_ref, o_ref, *, n_split):
    x = x_ref[...]
    mod = mod_ref[...]
    hn = _mod_norm(x, g_ref[...], mod[0:1], mod[1:2]).astype(BF16)
    f = w2_ref.shape[0]
    fc = f // n_split
    y = None
    for i in range(n_split):
        a = _dot(hn, w1_ref[:, i * fc:(i + 1) * fc])
        b = _dot(hn, w1_ref[:, f + i * fc:f + (i + 1) * fc])
        part = _dot((_silu(a) * b).astype(BF16), w2_ref[i * fc:(i + 1) * fc, :])
        y = part if y is None else y + part
    o_ref[...] = x + (FFN_RESIDUAL * mod[2:3]) * y


def _ffn(h, mod, gain, w1, w2, lc):
    n_batch, t, d = h.shape
    f = w2.shape[0]
    tm = _row_tile(lc, t - lc)
    n_split = 2 if (f // 2) % LANE == 0 else 1
    return pl.pallas_call(
        functools.partial(_ffn_kernel, n_split=n_split),
        grid=(n_batch, t // tm),
        in_specs=[
            pl.BlockSpec((None, tm, d), lambda b, i: (b, i, 0)),
            pl.BlockSpec((None, 3, d), _mod_index(n_batch, lc // tm)),
            pl.BlockSpec((1, d), lambda b, i: (0, 0)),
            _resident((d, 2 * f), lambda b, i: (0, 0)),
            _resident((f, d), lambda b, i: (0, 0)),
        ],
        out_specs=pl.BlockSpec((None, tm, d), lambda b, i: (b, i, 0)),
        out_shape=jax.ShapeDtypeStruct(h.shape, F32),
        compiler_params=_params("parallel", "parallel"),
        name="ffn",
    )(h, mod, gain.reshape(1, d), w1, w2)


def _proj_kernel(x_ref, mod_ref, g_ref, w_ref, o_ref):
    mod = mod_ref[...]
    hn = _mod_norm(x_ref[...], g_ref[...], mod[0:1], mod[1:2]).astype(BF16)
    o_ref[...] = _dot(hn, w_ref[...])


def _col_tile(n):
    tn = min(n, 2048)
    while n % tn:
        tn -= LANE
    return tn


def _proj(h, mod, gain, w, lc):
    n_batch, t, d = h.shape
    n = w.shape[1]
    tm = _row_tile(lc, t - lc)
    tn = _col_tile(n)
    mod_idx = _mod_index(n_batch, lc // tm)
    return pl.pallas_call(
        _proj_kernel,
        grid=(n // tn, n_batch, t // tm),
        in_specs=[
            pl.BlockSpec((None, tm, d), lambda j, b, i: (b, i, 0)),
            pl.BlockSpec((None, 3, d), lambda j, b, i: mod_idx(b, i)),
            pl.BlockSpec((1, d), lambda j, b, i: (0, 0)),
            pl.BlockSpec((d, tn), lambda j, b, i: (0, j)),
        ],
        out_specs=pl.BlockSpec((None, tm, tn), lambda j, b, i: (b, i, j)),
        out_shape=jax.ShapeDtypeStruct((n_batch, t, n), F32),
        compiler_params=_params("parallel", "parallel", "parallel"),
        name="mixer_in",
    )(h, mod, gain.reshape(1, d), w)


def _out_kernel(x_ref, y_ref, mod_ref, w_ref, o_ref):
    y = _dot(y_ref[...].astype(BF16), w_ref[...])
    o_ref[...] = x_ref[...] + mod_ref[2:3, :] * y


def _mixer_out(h, y, mod, w, lc):
    n_batch, t, d = h.shape
    k = y.shape[-1]
    tm = _row_tile(lc, t - lc)
    return pl.pallas_call(
        _out_kernel,
        grid=(n_batch, t // tm),
        in_specs=[
            pl.BlockSpec((None, tm, d), lambda b, i: (b, i, 0)),
            pl.BlockSpec((None, tm, k), lambda b, i: (b, i, 0)),
            pl.BlockSpec((None, 3, d), _mod_index(n_batch, lc // tm)),
            _resident((k, d), lambda b, i: (0, 0)),
        ],
        out_specs=pl.BlockSpec((None, tm, d), lambda b, i: (b, i, 0)),
        out_shape=jax.ShapeDtypeStruct(h.shape, F32),
        compiler_params=_params("parallel", "parallel"),
        name="mixer_out",
    )(h, y, mod, w)


def _unit_tri_inverse(a, blk_mask, eye_f):
    c = a.shape[0]
    d = jnp.where(blk_mask, a, 0.0)
    e = a - d
    p = eye_f - d
    dk = d
    n_sq = DN_INV_BLOCK.bit_length() - 2
    for _ in range(n_sq):
        dk = _mm(dk, dk)
        p = _mm(p, eye_f + dk)
    n = _mm(p, e)
    x = eye_f - n
    nk = n
    for _ in range((c // DN_INV_BLOCK).bit_length() - 2):
        nk = _mm(nk, nk)
        x = _mm(x, eye_f + nk)
    x = _mm(x, p)
    r = eye_f - x - _dot_x3(a, x)
    return x + _mm(x, r)


def _dn_kernel(alog_ref, dtb_ref, q_ref, k_ref, v_ref, z_ref, cwq_ref, cwk_ref, cwv_ref, ab_ref, og_ref,
               y_ref,
               pad_s, q_s, k_s, v_s, gb_s, u_s, w_s, qg_s, kd_s, qk_s, a_s, o_s, *, lc, n_heads):
    t = q_ref.shape[0]
    c = DN_CHUNK
    nc = t // c
    ncc = lc // c
    head = pl.program_id(1)
    rb = _row_tile(lc, t - lc)

    zeros8 = jnp.zeros((SUBLANE, DN_HEAD_DIM), F32)

    def conv_act(x_ref, cw_ref, dst, normalise):
        pad_s[0:SUBLANE, :] = zeros8
        pad_s[SUBLANE:SUBLANE + lc, :] = x_ref[0:lc, :]
        pad_s[SUBLANE + lc:2 * SUBLANE + lc, :] = zeros8
        pad_s[2 * SUBLANE + lc:2 * SUBLANE + t, :] = x_ref[lc:t, :]
        pad_s[2 * SUBLANE + t:3 * SUBLANE + t, :] = zeros8
        cw = cw_ref[...]
        n_tap = cw.shape[0]
        for r0 in range(0, t, rb):
            base = r0 + (SUBLANE if r0 < lc else 2 * SUBLANE)
            acc = None
            for j in range(n_tap):
                lo = base + j - n_tap // 2
                term = pad_s[lo:lo + rb, :] * cw[j:j + 1, :]
                acc = term if acc is None else acc + term
            act = _silu(acc)
            if normalise:
                act = act * lax.rsqrt(jnp.sum(act * act, axis=-1, keepdims=True) + EPS)
            dst[r0:r0 + rb, :] = act

    conv_act(q_ref, cwq_ref, q_s, True)
    conv_act(k_ref, cwk_ref, k_s, True)
    conv_act(v_ref, cwv_ref, v_s, False)

    col = lax.broadcasted_iota(jnp.int32, (1, 4), 1)
    alog_v = jnp.where(col < 2, alog_ref[head], alog_ref[n_heads + head])
    dtb_v = jnp.where(col < 2, dtb_ref[head], dtb_ref[n_heads + head])
    raw = ab_ref[...]
    gdec = -jnp.exp(alog_v) * _softplus(raw + dtb_v)
    gb_s[...] = jnp.where(col % 2 == 0, gdec, _sigmoid(raw))

    ri = lax.broadcasted_iota(jnp.int32, (c, c), 0)
    ci = lax.broadcasted_iota(jnp.int32, (c, c), 1)
    eye = ri == ci
    eye_f = jnp.where(eye, 1.0, 0.0).astype(F32)
    blk_mask = (ri // DN_INV_BLOCK) == (ci // DN_INV_BLOCK)
    ones_cc = jnp.ones((c, c), BF16)
    incl = (ci <= ri, ci >= ri)
    strict = (ci < ri, ci > ri)
    q_scale = DN_HEAD_DIM ** -0.5

    def phase1(ch, carry):
        r0 = pl.multiple_of(ch * c, c)
        q = q_s[pl.ds(r0, c), :] * q_scale
        k = k_s[pl.ds(r0, c), :]
        v = v_s[pl.ds(r0, c), :]
        gb = gb_s[pl.ds(r0, c), :]
        k_bf = k.astype(BF16)
        for d in range(2):
            g_col = gb[:, 2 * d:2 * d + 1]
            beta = gb[:, 2 * d + 1:2 * d + 2]
            tri = jnp.where(incl[d], 1.0, 0.0).astype(BF16)
            gcb = _dot_exact_lhs(tri, jnp.broadcast_to(g_col, (c, DN_HEAD_DIM)))
            gcr = _dot_exact_lhs(ones_cc, jnp.where(eye, gcb[:, :c], 0.0))
            decay = jnp.exp(jnp.where(incl[d], gcb[:, :c] - gcr, NEG_BIG))
            kb = k * beta
            kq = _dot_nt(jnp.concatenate([kb, q], axis=0).astype(BF16), k_bf)
            a = jnp.where(strict[d], kq[:c] * decay, 0.0)
            qk = kq[c:] * decay
            tinv = _unit_tri_inverse(a, blk_mask, eye_f)
            eg = jnp.exp(gcb)
            uw = _dot_x3(tinv, jnp.concatenate([v * beta, kb * eg], axis=1))
            g_tot = gcb[c - 1:c, :] if d == 0 else gcb[0:1, :]
            u_s[d, pl.ds(r0, c), :] = uw[:, :DN_HEAD_DIM]
            w_s[d, pl.ds(r0, c), :] = uw[:, DN_HEAD_DIM:].astype(BF16)
            qg_s[d, pl.ds(r0, c), :] = (q * eg).astype(BF16)
            kd_s[d, pl.ds(r0, c), :] = (k * jnp.exp(g_tot - gcb)).astype(BF16)
            qk_s[d, pl.ds(r0, c), :] = qk.astype(BF16)
            a_s[d, ch] = jnp.broadcast_to(jnp.exp(g_tot), (SUBLANE, DN_HEAD_DIM))
        return carry

    lax.fori_loop(0, nc, phase1, 0)

    def phase2(s, carry):
        chunk_of = (s, jnp.where(s < ncc, ncc - 1 - s, nc - 1 - s + ncc))
        new = []
        for d in range(2):
            state = carry[d]
            ch = chunk_of[d]
            r0 = pl.multiple_of(ch * c, c)
            wq = jnp.concatenate([w_s[d, pl.ds(r0, c), :], qg_s[d, pl.ds(r0, c), :]], axis=0)
            ws = _dot(wq, state.astype(BF16))
            v_new = (u_s[d, pl.ds(r0, c), :] - ws[:c]).astype(BF16)
            o_s[d, pl.ds(r0, c), :] = ws[c:] + _dot(qk_s[d, pl.ds(r0, c), :], v_new)
            new.append(state * a_s[d, ch][0:1, :] + _dot_tn(kd_s[d, pl.ds(r0, c), :], v_new))
        return tuple(new)

    zero_state = jnp.zeros((DN_HEAD_DIM, DN_HEAD_DIM), F32)
    lax.fori_loop(0, nc, phase2, (zero_state, zero_state))

    og = og_ref[...]
    for r0 in range(0, t, rb):
        o = o_s[0, r0:r0 + rb, :] + o_s[1, r0:r0 + rb, :]
        o = o * lax.rsqrt(jnp.mean(o * o, axis=-1, keepdims=True) + EPS) * og
        y_ref[r0:r0 + rb, :] = o * _silu(z_ref[r0:r0 + rb, :])


def _dn_core(qkvz, ab_cols, conv_w, a_log, dt_bias, o_gain, lc):
    n_batch, t, width = qkvz.shape
    hd = DN_HEAD_DIM
    nh = width // (4 * hd)
    nc = t // DN_CHUNK
    smem = pl.BlockSpec(memory_space=pltpu.SMEM)

    def col_block(offset):
        return pl.BlockSpec((None, t, hd), lambda b, h: (b, 0, offset + h))

    def conv_block(offset):
        return pl.BlockSpec((conv_w.shape[0], hd), lambda b, h: (0, offset + h))

    return pl.pallas_call(
        functools.partial(_dn_kernel, lc=lc, n_heads=nh),
        grid=(n_batch, nh),
        in_specs=[
            smem, smem,
            col_block(0), col_block(nh), col_block(2 * nh), col_block(3 * nh),
            conv_block(0), conv_block(nh), conv_block(2 * nh),
            pl.BlockSpec((None, None, t, 4), lambda b, h: (b, h, 0, 0)),
            pl.BlockSpec((1, hd), lambda b, h: (0, 0)),
        ],
        out_specs=pl.BlockSpec((None, t, hd), lambda b, h: (b, 0, h)),
        out_shape=jax.ShapeDtypeStruct((n_batch, t, nh * hd), F32),
        scratch_shapes=[
            pltpu.VMEM((t + 3 * SUBLANE, hd), F32),
            pltpu.VMEM((t, hd), F32),
            pltpu.VMEM((t, hd), F32),
            pltpu.VMEM((t, hd), F32),
            pltpu.VMEM((t, 4), F32),
            pltpu.VMEM((2, t, hd), F32),
            pltpu.VMEM((2, t, hd), BF16),
            pltpu.VMEM((2, t, hd), BF16),
            pltpu.VMEM((2, t, hd), BF16),
            pltpu.VMEM((2, t, DN_CHUNK), BF16),
            pltpu.VMEM((2, nc, SUBLANE, hd), F32),
            pltpu.VMEM((2, t, hd), F32),
        ],
        compiler_params=_params("parallel", "parallel"),
        name="deltanet",
    )(a_log.reshape(-1), dt_bias.reshape(-1), qkvz, qkvz, qkvz, qkvz, conv_w, conv_w, conv_w, ab_cols,
      o_gain.reshape(1, hd))


def _deltanet_mixer(h, mod, gain, w_in, conv_w, a_log, dt_bias, o_gain, w_out, lc):
    n_batch, t, d = h.shape
    nh = a_log.shape[-1]
    width = 4 * nh * DN_HEAD_DIM
    qkvz = _proj(h, mod, gain, w_in[:, :width].astype(BF16), lc)
    w_ab = jnp.pad(w_in[:, width:], ((0, 0), (0, LANE - 4 * nh))).astype(BF16)
    ab = _proj(h, mod, gain, w_ab, lc)[:, :, :4 * nh]
    ab_cols = ab.reshape(n_batch, t, 2, 2, nh).transpose(0, 4, 1, 2, 3).reshape(n_batch, nh, t, 4)
    y = _dn_core(qkvz, ab_cols, conv_w, a_log, dt_bias, o_gain, lc)
    return _mixer_out(h, y, mod, w_out.astype(BF16), lc)


def _headnorm_rope_kernel(x_ref, gain_ref, cos_ref, sin_ref, e_ref, et_ref, o_ref, *, scale):
    x = x_ref[...]
    width = x.shape[1]
    sq_hi, sq_lo = _split2(x * x)
    ss = _dot(sq_hi, e_ref[...]) + _dot(sq_lo, e_ref[...])
    inv_hi, inv_lo = _split2(lax.rsqrt(ss * (1.0 / SWA_HEAD_DIM) + EPS))
    inv = _dot(inv_hi, et_ref[...]) + _dot(inv_lo, et_ref[...])
    lane = lax.broadcasted_iota(jnp.int32, (1, LANE), 1)
    first_half = (lane % SWA_HEAD_DIM) < SWA_HEAD_DIM // 2
    gain = gain_ref[...]
    cos = cos_ref[...]
    sin = sin_ref[...]
    for s in range(width // LANE):
        xs = x[:, s * LANE:(s + 1) * LANE] * inv[:, s * LANE:(s + 1) * LANE] * gain
        partner = jnp.where(first_half, pltpu.roll(xs, LANE - SWA_HEAD_DIM // 2, 1),
                            pltpu.roll(xs, SWA_HEAD_DIM // 2, 1))
        o_ref[:, s * LANE:(s + 1) * LANE] = (xs * cos + partner * sin) * scale


def _headnorm_rope(x, gain, cos_t, sin_t, scale, lc):
    n_batch, t, width = x.shape
    tm = _row_tile(lc, t - lc)
    heads = lax.broadcasted_iota(jnp.int32, (width, LANE), 0) // SWA_HEAD_DIM
    e = (heads == lax.broadcasted_iota(jnp.int32, (width, LANE), 1)).astype(BF16)
    gain_t = jnp.tile(gain, LANE // SWA_HEAD_DIM).reshape(1, LANE)
    return pl.pallas_call(
        functools.partial(_headnorm_rope_kernel, scale=scale),
        grid=(n_batch, t // tm),
        in_specs=[
            pl.BlockSpec((None, tm, width), lambda b, i: (b, i, 0)),
            pl.BlockSpec((1, LANE), lambda b, i: (0, 0)),
            pl.BlockSpec((tm, LANE), lambda b, i: (i, 0)),
            pl.BlockSpec((tm, LANE), lambda b, i: (i, 0)),
            pl.BlockSpec((width, LANE), lambda b, i: (0, 0)),
            pl.BlockSpec((LANE, width), lambda b, i: (0, 0)),
        ],
        out_specs=pl.BlockSpec((None, tm, width), lambda b, i: (b, i, 0)),
        out_shape=jax.ShapeDtypeStruct(x.shape, F32),
        compiler_params=_params("parallel", "parallel"),
        name="headnorm_rope",
    )(x, gain_t, cos_t, sin_t, e, e.T)


def _attn_kernel(sink_ref, q_ref, k_ref, v_ref, o_ref, *, lc):
    t = k_ref.shape[0]
    hd = SWA_HEAD_DIM
    qb = ATTN_BLOCK
    span = qb + 2 * WINDOW
    n_kv = k_ref.shape[1] // hd
    j = pl.program_id(1)
    start = j * qb
    is_latent = start >= lc
    ws = pl.multiple_of(jnp.clip(start - WINDOW, lc, t - span), LANE)
    t_pos = start + lax.broadcasted_iota(jnp.int32, (qb, span), 0)
    s_pos = ws + lax.broadcasted_iota(jnp.int32, (qb, span), 1)
    allowed = jnp.abs(t_pos - s_pos) <= jnp.where(is_latent, WINDOW, -1)
    allowed = jnp.concatenate([allowed] * SWA_GROUP, axis=0)
    q = q_ref[...]
    kc = k_ref[0:lc, :].astype(BF16)
    vc = v_ref[0:lc, :].astype(BF16)
    kw = k_ref[pl.ds(ws, span), :].astype(BF16)
    vw = v_ref[pl.ds(ws, span), :].astype(BF16)
    for g in range(n_kv):
        cols = slice(g * hd, (g + 1) * hd)
        heads = [g * SWA_GROUP + i for i in range(SWA_GROUP)]
        qg = jnp.concatenate([q[:, hh * hd:(hh + 1) * hd] for hh in heads], axis=0).astype(BF16)
        sink = jnp.concatenate([jnp.full((qb, 1), sink_ref[hh], F32) for hh in heads], axis=0)
        s_ctx = _dot_nt(qg, kc[:, cols])
        s_win = jnp.where(allowed, _dot_nt(qg, kw[:, cols]), NEG_BIG)
        m = jnp.maximum(jnp.maximum(jnp.max(s_ctx, axis=-1, keepdims=True),
                                    jnp.max(s_win, axis=-1, keepdims=True)), sink)
        p_ctx = jnp.exp(s_ctx - m)
        p_win = jnp.exp(s_win - m)
        den = (jnp.sum(p_ctx, axis=-1, keepdims=True) + jnp.sum(p_win, axis=-1, keepdims=True)
               + jnp.exp(sink - m))
        o = (_dot(p_ctx.astype(BF16), vc[:, cols]) + _dot(p_win.astype(BF16), vw[:, cols])) / den
        for i, hh in enumerate(heads):
            o_ref[:, hh * hd:(hh + 1) * hd] = o[i * qb:(i + 1) * qb, :]


def _attention(q, k, v, sink, lc):
    n_batch, t, qw = q.shape
    kvw = k.shape[-1]
    return pl.pallas_call(
        functools.partial(_attn_kernel, lc=lc),
        grid=(n_batch, t // ATTN_BLOCK),
        in_specs=[
            pl.BlockSpec(memory_space=pltpu.SMEM),
            pl.BlockSpec((None, ATTN_BLOCK, qw), lambda b, j: (b, j, 0)),
            pl.BlockSpec((None, t, kvw), lambda b, j: (b, 0, 0)),
            pl.BlockSpec((None, t, kvw), lambda b, j: (b, 0, 0)),
        ],
        out_specs=pl.BlockSpec((None, ATTN_BLOCK, qw), lambda b, j: (b, j, 0)),
        out_shape=jax.ShapeDtypeStruct(q.shape, F32),
        compiler_params=_params("parallel", "parallel"),
        name="window_attention",
    )(sink, q, k, v)


def _swa_tables(lc, s):
    rows = jnp.repeat(jnp.arange(s // GRID_W), GRID_W).astype(F32)
    cols = jnp.tile(jnp.arange(GRID_W), s // GRID_W).astype(F32)
    n_freq = SWA_HEAD_DIM // 4
    inv = ROPE_BASE ** (-jnp.arange(n_freq, dtype=F32) / n_freq)
    ang = jnp.concatenate([rows[:, None] * inv[None, :], cols[:, None] * inv[None, :]], axis=-1)
    cos, sin = jnp.cos(ang), jnp.sin(ang)
    reps = LANE // SWA_HEAD_DIM
    cos_t = jnp.tile(jnp.concatenate([cos, cos], axis=-1), (1, reps))
    sin_t = jnp.tile(jnp.concatenate([-sin, sin], axis=-1), (1, reps))
    cos_t = jnp.concatenate([jnp.ones((lc, LANE), F32), cos_t], axis=0)
    sin_t = jnp.concatenate([jnp.zeros((lc, LANE), F32), sin_t], axis=0)
    return cos_t, sin_t


def _window_attention_mixer(h, mod, gain, w_qkv, q_gain, k_gain, sink, w_out, lc):
    n_batch, t, d = h.shape
    n_heads = sink.shape[0]
    q_w = n_heads * SWA_HEAD_DIM
    kv_w = (w_qkv.shape[1] - q_w) // 2
    cos_t, sin_t = _swa_tables(lc, t - lc)
    w_bf = w_qkv.astype(BF16)
    q = _proj(h, mod, gain, w_bf[:, :q_w], lc)
    k = _proj(h, mod, gain, w_bf[:, q_w:q_w + kv_w], lc)
    v = _proj(h, mod, gain, w_bf[:, q_w + kv_w:], lc)
    q = _headnorm_rope(q, q_gain, cos_t, sin_t, SWA_HEAD_DIM ** -0.5, lc)
    k = _headnorm_rope(k, k_gain, cos_t, sin_t, 1.0, lc)
    y = _attention(q, k, v, sink, lc)
    return _mixer_out(h, y, mod, w_out.astype(BF16), lc)


def _rope_halves(x, cos, sin):
    half = x.shape[1] // 2
    x1, x2 = x[:, :half], x[:, half:]
    return jnp.concatenate([x1 * cos - x2 * sin, x1 * sin + x2 * cos], axis=1)


def _ret_kernel(qf_ref, kf_ref, vf_ref, cosf_ref, sinf_ref, qb_ref, kb_ref, vb_ref, cosb_ref, sinb_ref,
                of_ref, ob_ref, sf_ref, sb_ref):
    c = RET_CHUNK
    rows = qf_ref.shape[0]
    dk = qf_ref.shape[1]
    head = pl.program_id(1)

    @pl.when(pl.program_id(2) == 0)
    def _():
        sf_ref[...] = jnp.zeros_like(sf_ref)
        sb_ref[...] = jnp.zeros_like(sb_ref)

    hv = jnp.zeros((1, 1), F32) + head.astype(F32)
    lg = jnp.log(1.0 - jnp.exp((-5.0 - hv) * math.log(2.0)))
    ri = lax.broadcasted_iota(jnp.int32, (c, c), 0)
    ci = lax.broadcasted_iota(jnp.int32, (c, c), 1)
    pos = lax.broadcasted_iota(jnp.int32, (c, 1), 0).astype(F32)
    chunk_decay = jnp.exp(lg * float(c))

    def run(q_ref, k_ref, v_ref, cos_ref, sin_ref, o_ref, s_ref, reverse):
        rel = ((ci - ri) if reverse else (ri - ci))
        decay = jnp.exp(jnp.where(rel >= 0, lg * rel.astype(F32), NEG_BIG))
        p = (c - 1.0 - pos) if reverse else pos
        q_fac = jnp.exp(lg * (p + 1.0))
        k_fac = jnp.exp(lg * (c - 1.0 - p))
        n_sub = rows // c
        order = range(n_sub - 1, -1, -1) if reverse else range(n_sub)
        for i in order:
            sl = slice(i * c, (i + 1) * c)
            q = _rope_halves(q_ref[sl, :], cos_ref[sl, :], sin_ref[sl, :])
            k = _rope_halves(k_ref[sl, :], cos_ref[sl, :], sin_ref[sl, :]) * (dk ** -0.5)
            v = v_ref[sl, :].astype(BF16)
            scores = _dot_nt(q.astype(BF16), k.astype(BF16)) * decay
            state = s_ref[...]
            o_ref[sl, :] = _dot(scores.astype(BF16), v) + _dot((q * q_fac).astype(BF16), state.astype(BF16))
            s_ref[...] = state * chunk_decay + _dot_tn((k * k_fac).astype(BF16), v)

    run(qf_ref, kf_ref, vf_ref, cosf_ref, sinf_ref, of_ref, sf_ref, False)
    run(qb_ref, kb_ref, vb_ref, cosb_ref, sinb_ref, ob_ref, sb_ref, True)


def _retention_core(proj, cos_t, sin_t, lc):
    n_batch, t, width = proj.shape
    nh = RET_HEADS
    dk = width // (8 * nh)
    dv = 2 * dk
    rows = lc
    n_steps = t // rows
    n_ctx = lc // rows

    def fwd(b, h, s):
        return s

    def bwd(b, h, s):
        return jnp.where(s < n_ctx, n_ctx - 1 - s, n_steps - 1 - s + n_ctx)

    def specs(step):
        return [
            pl.BlockSpec((None, rows, dk), lambda b, h, s: (b, step(b, h, s), h)),
            pl.BlockSpec((None, rows, dk), lambda b, h, s: (b, step(b, h, s), nh + h)),
            pl.BlockSpec((None, rows, dv), lambda b, h, s: (b, step(b, h, s), nh + h)),
            pl.BlockSpec((rows, dk // 2), lambda b, h, s: (step(b, h, s), 0)),
            pl.BlockSpec((rows, dk // 2), lambda b, h, s: (step(b, h, s), 0)),
        ]

    out_sds = jax.ShapeDtypeStruct((n_batch, t, nh * dv), F32)
    return pl.pallas_call(
        _ret_kernel,
        grid=(n_batch, nh, n_steps),
        in_specs=specs(fwd) + specs(bwd),
        out_specs=[
            pl.BlockSpec((None, rows, dv), lambda b, h, s: (b, fwd(b, h, s), h)),
            pl.BlockSpec((None, rows, dv), lambda b, h, s: (b, bwd(b, h, s), h)),
        ],
        out_shape=[out_sds, out_sds],
        scratch_shapes=[pltpu.VMEM((dk, dv), F32), pltpu.VMEM((dk, dv), F32)],
        compiler_params=_params("parallel", "parallel", "arbitrary"),
        name="retention",
    )(proj, proj, proj, cos_t, sin_t, proj, proj, proj, cos_t, sin_t)


def _ret_out_kernel(x_ref, of_ref, ob_ref, gf_ref, gb_ref, gain_ref, mod_ref, w_ref, o_ref, *, n_heads):
    dv = of_ref.shape[1] // n_heads
    parts = []
    for hh in range(n_heads):
        sl = slice(hh * dv, (hh + 1) * dv)

        def group_norm(o, gain):
            mu = jnp.mean(o, axis=-1, keepdims=True)
            xc = o - mu
            var = jnp.mean(xc * xc, axis=-1, keepdims=True)
            return xc * lax.rsqrt(var + EPS) * gain

        y = (_silu(gf_ref[:, sl]) * group_norm(of_ref[:, sl], gain_ref[0:1, sl])
             + _silu(gb_ref[:, sl]) * group_norm(ob_ref[:, sl], gain_ref[1:2, sl]))
        parts.append(y.astype(BF16))
    y = _dot(jnp.concatenate(parts, axis=1), w_ref[...])
    o_ref[...] = x_ref[...] + mod_ref[2:3, :] * y


def _retention_out(h, o_f, o_b, proj, gn_gain, mod, w, lc):
    n_batch, t, d = h.shape
    vw = o_f.shape[-1]
    tm = _row_tile(lc, t - lc)
    gate_f = proj.shape[-1] // vw - 2
    return pl.pallas_call(
        functools.partial(_ret_out_kernel, n_heads=RET_HEADS),
        grid=(n_batch, t // tm),
        in_specs=[
            pl.BlockSpec((None, tm, d), lambda b, i: (b, i, 0)),
            pl.BlockSpec((None, tm, vw), lambda b, i: (b, i, 0)),
            pl.BlockSpec((None, tm, vw), lambda b, i: (b, i, 0)),
            pl.BlockSpec((None, tm, vw), lambda b, i: (b, i, gate_f)),
            pl.BlockSpec((None, tm, vw), lambda b, i: (b, i, gate_f + 1)),
            pl.BlockSpec((2, vw), lambda b, i: (0, 0)),
            pl.BlockSpec((None, 3, d), _mod_index(n_batch, lc // tm)),
            _resident((vw, d), lambda b, i: (0, 0)),
        ],
        out_specs=pl.BlockSpec((None, tm, d), lambda b, i: (b, i, 0)),
        out_shape=jax.ShapeDtypeStruct(h.shape, F32),
        compiler_params=_params("parallel", "parallel"),
        name="retention_out",
    )(h, o_f, o_b, proj, proj, gn_gain, mod, w)


def _ret_tables(lc, s, n_freq):
    inv = ROPE_BASE ** (-jnp.arange(n_freq, dtype=F32) / n_freq)
    ang = jnp.arange(s, dtype=F32)[:, None] * inv[None, :]
    cos_t = jnp.concatenate([jnp.ones((lc, n_freq), F32), jnp.cos(ang)], axis=0)
    sin_t = jnp.concatenate([jnp.zeros((lc, n_freq), F32), jnp.sin(ang)], axis=0)
    return cos_t, sin_t


def _retention_mixer(h, mod, gain, w_in, gn_gain, w_out, lc):
    n_batch, t, d = h.shape
    dk = w_in.shape[1] // (8 * RET_HEADS)
    cos_t, sin_t = _ret_tables(lc, t - lc, dk // 2)
    proj = _proj(h, mod, gain, w_in.astype(BF16), lc)
    o_f, o_b = _retention_core(proj, cos_t, sin_t, lc)
    return _retention_out(h, o_f, o_b, proj, gn_gain, mod, w_out.astype(BF16), lc)


def kernel(x, c, ctx, c_ctx, ada_w, ada_b, norm_g, ffn_w1, ffn_w2, dn_w_in, dn_conv, dn_a_log, dn_dt_bias,
           dn_o_gain, dn_w_out, swa_w_qkv, swa_q_gain, swa_k_gain, swa_sink, swa_w_out, ret_w_in, ret_gn_gain,
           ret_w_out):
    n_batch, s, d = x.shape
    lc = ctx.shape[1]
    depth = ada_w.shape[0]
    n_mod = ada_w.shape[2] // d

    bp = -(-(n_batch + 1) // SUBLANE) * SUBLANE
    cond = jnp.concatenate([c, c_ctx[None, :], jnp.zeros((bp - n_batch - 1, d), F32)], axis=0)
    mods = _ada_mods(cond, ada_w, ada_b).reshape(depth, bp, n_mod, d)

    h = jnp.concatenate([ctx, x], axis=1)
    w1 = ffn_w1.astype(BF16)
    w2 = ffn_w2.astype(BF16)
    for i in range(depth):
        kind, slot = i % N_MIXERS, i // N_MIXERS
        sub = [mods[i, :, 3 * j:3 * j + 3, :] for j in range(3)]
        h = _ffn(h, sub[0], norm_g[i, 0], w1[i, 0], w2[i, 0], lc)
        if kind == 0:
            h = _deltanet_mixer(h, sub[1], norm_g[i, 1], dn_w_in[slot], dn_conv[slot], dn_a_log[slot],
                                dn_dt_bias[slot], dn_o_gain[slot], dn_w_out[slot], lc)
        elif kind == 1:
            h = _window_attention_mixer(h, sub[1], norm_g[i, 1], swa_w_qkv[slot], swa_q_gain[slot],
                                        swa_k_gain[slot], swa_sink[slot], swa_w_out[slot], lc)
        else:
            h = _retention_mixer(h, sub[1], norm_g[i, 1], ret_w_in[slot], ret_gn_gain[slot], ret_w_out[slot], lc)
        h = _ffn(h, sub[2], norm_g[i, 2], w1[i, 1], w2[i, 1], lc)
    return h[:, lc:, :]
```

```python
import functools
import math

import jax
import jax.numpy as jnp
from jax import lax
from jax.experimental import pallas as pl
from jax.experimental.pallas import tpu as pltpu

F32 = jnp.float32
BF16 = jnp.bfloat16

N_MIXERS = 3
GRID_W = 64
ROPE_BASE = 10000.0
EPS = 1e-6
FFN_RESIDUAL = 0.5
DN_HEAD_DIM = 128
DN_CHUNK = 64
DN_INV_BLOCK = 16
DN_CHUNK_BATCHES = (12, 6, 5, 4, 3, 2, 1)
SWA_HEAD_DIM = 64
SWA_GROUP = 4
WINDOW = 128
ATTN_BLOCK = 128
RET_HEADS = 4
RET_CHUNK = 128

LANE = 128
SUBLANE = 8
VMEM_LIMIT_BYTES = 52 * 1024 * 1024
NEG_BIG = -1e30


def _params(*semantics):
    return pltpu.CompilerParams(dimension_semantics=semantics, vmem_limit_bytes=VMEM_LIMIT_BYTES)


def _resident(block_shape, index_map):
    return pl.BlockSpec(block_shape, index_map, pipeline_mode=pl.Buffered(1))


def _dot(a, b):
    return jnp.dot(a, b, preferred_element_type=F32)


def _dot_nt(a, b):
    return lax.dot_general(a, b, (((1,), (1,)), ((), ())), preferred_element_type=F32)


def _dot_tn(a, b):
    return lax.dot_general(a, b, (((0,), (0,)), ((), ())), preferred_element_type=F32)


def _split2(x):
    hi = x.astype(BF16)
    lo = (x - hi.astype(F32)).astype(BF16)
    return hi, lo


def _split3(x):
    hi = x.astype(BF16)
    r = x - hi.astype(F32)
    mid = r.astype(BF16)
    lo = (r - mid.astype(F32)).astype(BF16)
    return hi, mid, lo


def _dot_x3(a, b):
    ah, al = _split2(a)
    bh, bl = _split2(b)
    return _dot(ah, bh) + (_dot(ah, bl) + _dot(al, bh))


def _sigmoid(x):
    return 1.0 / (1.0 + jnp.exp(-x))


def _silu(x):
    return x * _sigmoid(x)


def _softplus(x):
    return jnp.maximum(x, 0.0) + jnp.log(1.0 + jnp.exp(-jnp.abs(x)))


def _mod_norm(x, gain, shift, scale):
    ms = jnp.mean(x * x, axis=-1, keepdims=True)
    return (x * lax.rsqrt(ms + EPS)) * (gain * (1.0 + scale)) + shift


def _ada_kernel(c_ref, w_ref, b_ref, o_ref):
    o_ref[...] = _dot_x3(_silu(c_ref[...]), w_ref[...]) + b_ref[...]


def _ada_mods(cond, ada_w, ada_b):
    depth, d, nd = ada_w.shape
    bp = cond.shape[0]
    tn = d
    return pl.pallas_call(
        _ada_kernel,
        grid=(depth, nd // tn),
        in_specs=[
            pl.BlockSpec((bp, d), lambda l, n: (0, 0)),
            pl.BlockSpec((None, d, tn), lambda l, n: (l, 0, n)),
            pl.BlockSpec((None, 1, tn), lambda l, n: (l, 0, n)),
        ],
        out_specs=pl.BlockSpec((None, bp, tn), lambda l, n: (l, 0, n)),
        out_shape=jax.ShapeDtypeStruct((depth, bp, nd), F32),
        compiler_params=_params("parallel", "parallel"),
        name="ada_mods",
    )(cond, ada_w, ada_b.reshape(depth, 1, nd))


def _row_tile(lc, s):
    tm = 256
    while lc % tm or s % tm:
        tm //= 2
    return tm


def _mod_index(n_batch, n_ctx_tiles):
    return lambda b, t: (jnp.where(t < n_ctx_tiles, n_batch, b), 0, 0)


def _ffn_kernel(x_ref, mod_ref, g_ref, w1_ref, w2_ref, o_ref, *, n_split):
    x = x_ref[...]
    mod = mod_ref[...]
    hn = _mod_norm(x, g_ref[...], mod[0:1], mod[1:2]).astype(BF16)
    f = w2_ref.shape[0]
    fc = f // n_split
    y = None
    for i in range(n_split):
        a = _dot(hn, w1_ref[:, i * fc:(i + 1) * fc])
        b = _dot(hn, w1_ref[:, f + i * fc:f + (i + 1) * fc])
        part = _dot((_silu(a) * b).astype(BF16), w2_ref[i * fc:(i + 1) * fc, :])
        y = part if y is None else y + part
    o_ref[...] = x + (FFN_RESIDUAL * mod[2:3]) * y


def _ffn(h, mod, gain, w1, w2, lc):
    n_batch, t, d = h.shape
    f = w2.shape[0]
    tm = _row_tile(lc, t - lc)
    n_split = 2 if (f // 2) % LANE == 0 else 1
    return pl.pallas_call(
        functools.partial(_ffn_kernel, n_split=n_split),
        grid=(n_batch, t // tm),
        in_specs=[
            pl.BlockSpec((None, tm, d), lambda b, i: (b, i, 0)),
            pl.BlockSpec((None, 3, d), _mod_index(n_batch, lc // tm)),
            pl.BlockSpec((1, d), lambda b, i: (0, 0)),
            _resident((d, 2 * f), lambda b, i: (0, 0)),
            _resident((f, d), lambda b, i: (0, 0)),
        ],
        out_specs=pl.BlockSpec((None, tm, d), lambda b, i: (b, i, 0)),
        out_shape=jax.ShapeDtypeStruct(h.shape, F32),
        compiler_params=_params("parallel", "parallel"),
        name="ffn",
    )(h, mod, gain.reshape(1, d), w1, w2)


def _proj_kernel(x_ref, mod_ref, g_ref, w_ref, o_ref):
    mod = mod_ref[...]
    hn = _mod_norm(x_ref[...], g_ref[...], mod[0:1], mod[1:2]).astype(BF16)
    o_ref[...] = _dot(hn, w_ref[...])


def _col_tile(n):
    tn = min(n, 2048)
    while n % tn:
        tn -= LANE
    return tn


def _proj(h, mod, gain, w, lc):
    n_batch, t, d = h.shape
    n = w.shape[1]
    tm = _row_tile(lc, t - lc)
    tn = _col_tile(n)
    mod_idx = _mod_index(n_batch, lc // tm)
    return pl.pallas_call(
        _proj_kernel,
        grid=(n // tn, n_batch, t // tm),
        in_specs=[
            pl.BlockSpec((None, tm, d), lambda j, b, i: (b, i, 0)),
            pl.BlockSpec((None, 3, d), lambda j, b, i: mod_idx(b, i)),
            pl.BlockSpec((1, d), lambda j, b, i: (0, 0)),
            pl.BlockSpec((d, tn), lambda j, b, i: (0, j)),
        ],
        out_specs=pl.BlockSpec((None, tm, tn), lambda j, b, i: (b, i, j)),
        out_shape=jax.ShapeDtypeStruct((n_batch, t, n), F32),
        compiler_params=_params("parallel", "parallel", "parallel"),
        name="mixer_in",
    )(h, mod, gain.reshape(1, d), w)


def _out_kernel(x_ref, y_ref, mod_ref, w_ref, o_ref):
    y = _dot(y_ref[...].astype(BF16), w_ref[...])
    o_ref[...] = x_ref[...] + mod_ref[2:3, :] * y


def _mixer_out(h, y, mod, w, lc):
    n_batch, t, d = h.shape
    k = y.shape[-1]
    tm = _row_tile(lc, t - lc)
    return pl.pallas_call(
        _out_kernel,
        grid=(n_batch, t // tm),
        in_specs=[
            pl.BlockSpec((None, tm, d), lambda b, i: (b, i, 0)),
            pl.BlockSpec((None, tm, k), lambda b, i: (b, i, 0)),
            pl.BlockSpec((None, 3, d), _mod_index(n_batch, lc // tm)),
            _resident((k, d), lambda b, i: (0, 0)),
        ],
        out_specs=pl.BlockSpec((None, tm, d), lambda b, i: (b, i, 0)),
        out_shape=jax.ShapeDtypeStruct(h.shape, F32),
        compiler_params=_params("parallel", "parallel"),
        name="mixer_out",
    )(h, y, mod, w)


def _bmm(a, b):
    return jnp.einsum("gik,gkj->gij", a.astype(BF16), b.astype(BF16), preferred_element_type=F32)


def _unit_tri_inverse(a, blk_mask, eye_f):
    c = a.shape[-1]
    d = jnp.where(blk_mask, a, 0.0)
    e = a - d
    p = eye_f - d
    dk = d
    for _ in range(DN_INV_BLOCK.bit_length() - 2):
        dk = _bmm(dk, dk)
        p = _bmm(p, eye_f + dk)
    n = _bmm(p, e)
    x = eye_f - n
    nk = n
    for _ in range((c // DN_INV_BLOCK).bit_length() - 2):
        nk = _bmm(nk, nk)
        x = _bmm(x, eye_f + nk)
    return _bmm(x, p)


def _dn_kernel(q_ref, k_ref, v_ref, z_ref, cwq_ref, cwk_ref, cwv_ref, gcol_ref, grow_ref, og_ref,
               y_ref,
               pad_s, q_s, k_s, v_s, kq_s, b_s, a_s, o_s, *, lc):
    t = q_ref.shape[0]
    c = DN_CHUNK
    nc = t // c
    ncc = lc // c
    rb = _row_tile(lc, t - lc)

    zeros8 = jnp.zeros((SUBLANE, DN_HEAD_DIM), F32)

    def conv_act(x_ref, cw_ref, dst, normalise):
        pad_s[0:SUBLANE, :] = zeros8
        pad_s[SUBLANE:SUBLANE + lc, :] = x_ref[0:lc, :]
        pad_s[SUBLANE + lc:2 * SUBLANE + lc, :] = zeros8
        pad_s[2 * SUBLANE + lc:2 * SUBLANE + t, :] = x_ref[lc:t, :]
        pad_s[2 * SUBLANE + t:3 * SUBLANE + t, :] = zeros8
        cw = cw_ref[...]
        n_tap = cw.shape[0]
        for r0 in range(0, t, rb):
            base = r0 + (SUBLANE if r0 < lc else 2 * SUBLANE)
            acc = None
            for j in range(n_tap):
                lo = base + j - n_tap // 2
                term = pad_s[lo:lo + rb, :] * cw[j:j + 1, :]
                acc = term if acc is None else acc + term
            act = _silu(acc)
            if normalise:
                act = act * lax.rsqrt(jnp.sum(act * act, axis=-1, keepdims=True) + EPS)
            dst[r0:r0 + rb, :] = act

    conv_act(q_ref, cwq_ref, q_s, True)
    conv_act(k_ref, cwk_ref, k_s, True)
    conv_act(v_ref, cwv_ref, v_s, False)

    ri = lax.broadcasted_iota(jnp.int32, (c, c), 0)
    ci = lax.broadcasted_iota(jnp.int32, (c, c), 1)
    eye_f = jnp.where(ri == ci, 1.0, 0.0).astype(F32)
    blk_mask = (ri // DN_INV_BLOCK) == (ci // DN_INV_BLOCK)
    incl = (ci <= ri, ci >= ri)
    strict = (ci < ri, ci > ri)
    q_scale = DN_HEAD_DIM ** -0.5
    g = max(n for n in DN_CHUNK_BATCHES if nc % n == 0)

    def phase1(it, carry):
        ch0 = it * g
        rows = pl.ds(pl.multiple_of(ch0 * c, c), g * c)
        q = q_s[rows, :].reshape(g, c, DN_HEAD_DIM) * q_scale
        k = k_s[rows, :].reshape(g, c, DN_HEAD_DIM)
        v = v_s[rows, :].reshape(g, c, DN_HEAD_DIM)
        gcol = gcol_ref[rows, :].reshape(g, c, 4)
        grow = grow_ref[pl.ds(ch0, g)]
        k_bf = k.astype(BF16)
        for d in range(2):
            gc = gcol[:, :, 2 * d:2 * d + 1]
            beta = gcol[:, :, 2 * d + 1:2 * d + 2]
            decay = jnp.exp(jnp.where(incl[d], gc - grow[:, d:d + 1, :], NEG_BIG))
            kb = k * beta
            kq = jnp.einsum("gic,gjc->gij", jnp.concatenate([kb, q], axis=1).astype(BF16), k_bf,
                            preferred_element_type=F32)
            a = jnp.where(strict[d], kq[:, :c] * decay, 0.0)
            qk = kq[:, c:] * decay
            tinv = _unit_tri_inverse(a, blk_mask, eye_f)
            eg = jnp.exp(gc)
            wu = _bmm(tinv, jnp.concatenate([kb * eg, v * beta], axis=2)).astype(BF16)
            g_tot = gc[:, c - 1:c, :] if d == 0 else gc[:, 0:1, :]
            kd = (k * jnp.exp(g_tot - gc)).astype(BF16)
            kwu = jnp.einsum("gck,gcn->gkn", kd, wu, preferred_element_type=F32)
            qwu = _bmm(qk, wu)
            kq_s[d, pl.ds(ch0, g), 0:DN_HEAD_DIM, :] = kwu[:, :, :DN_HEAD_DIM].astype(BF16)
            kq_s[d, pl.ds(ch0, g), DN_HEAD_DIM:, :] = (q * eg - qwu[:, :, :DN_HEAD_DIM]).astype(BF16)
            b_s[d, pl.ds(ch0, g)] = kwu[:, :, DN_HEAD_DIM:]
            o_s[d, rows, :] = qwu[:, :, DN_HEAD_DIM:].reshape(g * c, DN_HEAD_DIM)
            a_s[d, pl.ds(ch0, g)] = jnp.broadcast_to(jnp.exp(g_tot), (g, SUBLANE, DN_HEAD_DIM))
        return carry

    lax.fori_loop(0, nc // g, phase1, 0)

    def phase2(s, carry):
        chunk_of = (s, jnp.where(s < ncc, ncc - 1 - s, nc - 1 - s + ncc))
        new = []
        for d in range(2):
            state = carry[d]
            ch = chunk_of[d]
            rows = pl.ds(pl.multiple_of(ch * c, c), c)
            r = _dot(kq_s[d, ch], state.astype(BF16))
            o_s[d, rows, :] = o_s[d, rows, :] + r[DN_HEAD_DIM:]
            new.append(state * a_s[d, ch][0:1, :] + (b_s[d, ch] - r[:DN_HEAD_DIM]))
        return tuple(new)

    zero_state = jnp.zeros((DN_HEAD_DIM, DN_HEAD_DIM), F32)
    lax.fori_loop(0, nc, phase2, (zero_state, zero_state))

    og = og_ref[...]
    for r0 in range(0, t, rb):
        o = o_s[0, r0:r0 + rb, :] + o_s[1, r0:r0 + rb, :]
        o = o * lax.rsqrt(jnp.mean(o * o, axis=-1, keepdims=True) + EPS) * og
        y_ref[r0:r0 + rb, :] = o * _silu(z_ref[r0:r0 + rb, :])


def _dn_gates_kernel(ab_ref, alog_ref, dtb_ref, o_ref, *, n_heads):
    t, width = ab_ref.shape
    c = DN_CHUNK
    col = lax.broadcasted_iota(jnp.int32, (1, width), 1)
    is_decay = (col % (2 * n_heads)) < n_heads
    backward = col >= 2 * n_heads
    ri = lax.broadcasted_iota(jnp.int32, (c, c), 0)
    ci = lax.broadcasted_iota(jnp.int32, (c, c), 1)
    tri_f = jnp.where(ci <= ri, 1.0, 0.0).astype(BF16)
    tri_b = jnp.where(ci >= ri, 1.0, 0.0).astype(BF16)
    neg_a = -jnp.exp(alog_ref[...])
    dtb = dtb_ref[...]
    for r0 in range(0, t, c):
        raw = ab_ref[r0:r0 + c, :]
        g = neg_a * _softplus(raw + dtb)
        g1, g2, g3 = _split3(g)
        cum_f = _dot(tri_f, g1) + (_dot(tri_f, g2) + _dot(tri_f, g3))
        cum_b = _dot(tri_b, g1) + (_dot(tri_b, g2) + _dot(tri_b, g3))
        o_ref[r0:r0 + c, :] = jnp.where(is_decay, jnp.where(backward, cum_b, cum_f), _sigmoid(raw))


def _dn_gates(ab, a_log, dt_bias):
    n_batch, t, width = ab.shape
    nh = a_log.shape[-1]

    def param_row(p):
        z = jnp.zeros((nh,), F32)
        row = jnp.concatenate([p[0], z, p[1], z])
        return jnp.pad(row, (0, width - 4 * nh)).reshape(1, width)

    return pl.pallas_call(
        functools.partial(_dn_gates_kernel, n_heads=nh),
        grid=(n_batch,),
        in_specs=[
            pl.BlockSpec((None, t, width), lambda b: (b, 0, 0)),
            pl.BlockSpec((1, width), lambda b: (0, 0)),
            pl.BlockSpec((1, width), lambda b: (0, 0)),
        ],
        out_specs=pl.BlockSpec((None, t, width), lambda b: (b, 0, 0)),
        out_shape=jax.ShapeDtypeStruct(ab.shape, F32),
        compiler_params=_params("parallel"),
        name="deltanet_gates",
    )(ab, param_row(a_log), param_row(dt_bias))


def _dn_core(qkvz, gate_cols, gate_rows, conv_w, o_gain, lc):
    n_batch, t, width = qkvz.shape
    hd = DN_HEAD_DIM
    nh = width // (4 * hd)
    nc = t // DN_CHUNK

    def col_block(offset):
        return pl.BlockSpec((None, t, hd), lambda b, h: (b, 0, offset + h))

    def conv_block(offset):
        return pl.BlockSpec((conv_w.shape[0], hd), lambda b, h: (0, offset + h))

    return pl.pallas_call(
        functools.partial(_dn_kernel, lc=lc),
        grid=(n_batch, nh),
        in_specs=[
            col_block(0), col_block(nh), col_block(2 * nh), col_block(3 * nh),
            conv_block(0), conv_block(nh), conv_block(2 * nh),
            pl.BlockSpec((None, None, t, 4), lambda b, h: (b, h, 0, 0)),
            pl.BlockSpec((None, None, nc, 2, DN_CHUNK), lambda b, h: (b, h, 0, 0, 0)),
            pl.BlockSpec((1, hd), lambda b, h: (0, 0)),
        ],
        out_specs=pl.BlockSpec((None, t, hd), lambda b, h: (b, 0, h)),
        out_shape=jax.ShapeDtypeStruct((n_batch, t, nh * hd), F32),
        scratch_shapes=[
            pltpu.VMEM((t + 3 * SUBLANE, hd), F32),
            pltpu.VMEM((t, hd), F32),
            pltpu.VMEM((t, hd), F32),
            pltpu.VMEM((t, hd), F32),
            pltpu.VMEM((2, nc, hd + DN_CHUNK, hd), BF16),
            pltpu.VMEM((2, nc, hd, hd), F32),
            pltpu.VMEM((2, nc, SUBLANE, hd), F32),
            pltpu.VMEM((2, t, hd), F32),
        ],
        compiler_params=_params("parallel", "parallel"),
        name="deltanet",
    )(qkvz, qkvz, qkvz, qkvz, conv_w, conv_w, conv_w, gate_cols, gate_rows, o_gain.reshape(1, hd))


def _deltanet_mixer(h, mod, gain, w_in, conv_w, a_log, dt_bias, o_gain, w_out, lc):
    n_batch, t, d = h.shape
    nh = a_log.shape[-1]
    width = 4 * nh * DN_HEAD_DIM
    qkvz = _proj(h, mod, gain, w_in[:, :width].astype(BF16), lc)
    w_ab = jnp.pad(w_in[:, width:], ((0, 0), (0, LANE - 4 * nh))).astype(BF16)
    gates = _dn_gates(_proj(h, mod, gain, w_ab, lc), a_log, dt_bias)[:, :, :4 * nh]
    gates = gates.reshape(n_batch, t, 2, 2, nh)
    gate_cols = gates.transpose(0, 4, 1, 2, 3).reshape(n_batch, nh, t, 4)
    gate_rows = (gates[:, :, :, 0, :].transpose(0, 3, 2, 1)
                 .reshape(n_batch, nh, 2, t // DN_CHUNK, DN_CHUNK).transpose(0, 1, 3, 2, 4))
    y = _dn_core(qkvz, gate_cols, gate_rows, conv_w, o_gain, lc)
    return _mixer_out(h, y, mod, w_out.astype(BF16), lc)


def _headnorm_rope_kernel(x_ref, gain_ref, cos_ref, sin_ref, e_ref, et_ref, o_ref, *, scale):
    x = x_ref[...]
    width = x.shape[1]
    sq_hi, sq_lo = _split2(x * x)
    ss = _dot(sq_hi, e_ref[...]) + _dot(sq_lo, e_ref[...])
    inv_hi, inv_lo = _split2(lax.rsqrt(ss * (1.0 / SWA_HEAD_DIM) + EPS))
    inv = _dot(inv_hi, et_ref[...]) + _dot(inv_lo, et_ref[...])
    lane = lax.broadcasted_iota(jnp.int32, (1, LANE), 1)
    first_half = (lane % SWA_HEAD_DIM) < SWA_HEAD_DIM // 2
    gain = gain_ref[...]
    cos = cos_ref[...]
    sin = sin_ref[...]
    for s in range(width // LANE):
        xs = x[:, s * LANE:(s + 1) * LANE] * inv[:, s * LANE:(s + 1) * LANE] * gain
        partner = jnp.where(first_half, pltpu.roll(xs, LANE - SWA_HEAD_DIM // 2, 1),
                            pltpu.roll(xs, SWA_HEAD_DIM // 2, 1))
        o_ref[:, s * LANE:(s + 1) * LANE] = (xs * cos + partner * sin) * scale


def _headnorm_rope(x, gain, cos_t, sin_t, scale, lc):
    n_batch, t, width = x.shape
    tm = _row_tile(lc, t - lc)
    heads = lax.broadcasted_iota(jnp.int32, (width, LANE), 0) // SWA_HEAD_DIM
    e = (heads == lax.broadcasted_iota(jnp.int32, (width, LANE), 1)).astype(BF16)
    gain_t = jnp.tile(gain, LANE // SWA_HEAD_DIM).reshape(1, LANE)
    return pl.pallas_call(
        functools.partial(_headnorm_rope_kernel, scale=scale),
        grid=(n_batch, t // tm),
        in_specs=[
            pl.BlockSpec((None, tm, width), lambda b, i: (b, i, 0)),
            pl.BlockSpec((1, LANE), lambda b, i: (0, 0)),
            pl.BlockSpec((tm, LANE), lambda b, i: (i, 0)),
            pl.BlockSpec((tm, LANE), lambda b, i: (i, 0)),
            pl.BlockSpec((width, LANE), lambda b, i: (0, 0)),
            pl.BlockSpec((LANE, width), lambda b, i: (0, 0)),
        ],
        out_specs=pl.BlockSpec((None, tm, width), lambda b, i: (b, i, 0)),
        out_shape=jax.ShapeDtypeStruct(x.shape, F32),
        compiler_params=_params("parallel", "parallel"),
        name="headnorm_rope",
    )(x, gain_t, cos_t, sin_t, e, e.T)


def _attn_kernel(sink_ref, q_ref, k_ref, v_ref, o_ref, *, lc):
    t = k_ref.shape[0]
    hd = SWA_HEAD_DIM
    qb = ATTN_BLOCK
    span = qb + 2 * WINDOW
    n_kv = k_ref.shape[1] // hd
    j = pl.program_id(1)
    start = j * qb
    is_latent = start >= lc
    ws = pl.multiple_of(jnp.clip(start - WINDOW, lc, t - span), LANE)
    t_pos = start + lax.broadcasted_iota(jnp.int32, (qb, span), 0)
    s_pos = ws + lax.broadcasted_iota(jnp.int32, (qb, span), 1)
    allowed = jnp.abs(t_pos - s_pos) <= jnp.where(is_latent, WINDOW, -1)
    allowed = jnp.concatenate([allowed] * SWA_GROUP, axis=0)
    q = q_ref[...]
    kc = k_ref[0:lc, :].astype(BF16)
    vc = v_ref[0:lc, :].astype(BF16)
    kw = k_ref[pl.ds(ws, span), :].astype(BF16)
    vw = v_ref[pl.ds(ws, span), :].astype(BF16)
    for g in range(n_kv):
        cols = slice(g * hd, (g + 1) * hd)
        heads = [g * SWA_GROUP + i for i in range(SWA_GROUP)]
        qg = jnp.concatenate([q[:, hh * hd:(hh + 1) * hd] for hh in heads], axis=0).astype(BF16)
        sink = jnp.concatenate([jnp.full((qb, 1), sink_ref[hh], F32) for hh in heads], axis=0)
        s_ctx = _dot_nt(qg, kc[:, cols])
        s_win = jnp.where(allowed, _dot_nt(qg, kw[:, cols]), NEG_BIG)
        m = jnp.maximum(jnp.maximum(jnp.max(s_ctx, axis=-1, keepdims=True),
                                    jnp.max(s_win, axis=-1, keepdims=True)), sink)
        p_ctx = jnp.exp(s_ctx - m)
        p_win = jnp.exp(s_win - m)
        den = (jnp.sum(p_ctx, axis=-1, keepdims=True) + jnp.sum(p_win, axis=-1, keepdims=True)
               + jnp.exp(sink - m))
        o = (_dot(p_ctx.astype(BF16), vc[:, cols]) + _dot(p_win.astype(BF16), vw[:, cols])) / den
        for i, hh in enumerate(heads):
            o_ref[:, hh * hd:(hh + 1) * hd] = o[i * qb:(i + 1) * qb, :]


def _attention(q, k, v, sink, lc):
    n_batch, t, qw = q.shape
    kvw = k.shape[-1]
    return pl.pallas_call(
        functools.partial(_attn_kernel, lc=lc),
        grid=(n_batch, t // ATTN_BLOCK),
        in_specs=[
            pl.BlockSpec(memory_space=pltpu.SMEM),
            pl.BlockSpec((None, ATTN_BLOCK, qw), lambda b, j: (b, j, 0)),
            pl.BlockSpec((None, t, kvw), lambda b, j: (b, 0, 0)),
            pl.BlockSpec((None, t, kvw), lambda b, j: (b, 0, 0)),
        ],
        out_specs=pl.BlockSpec((None, ATTN_BLOCK, qw), lambda b, j: (b, j, 0)),
        out_shape=jax.ShapeDtypeStruct(q.shape, F32),
        compiler_params=_params("parallel", "parallel"),
        name="window_attention",
    )(sink, q, k, v)


def _swa_tables(lc, s):
    rows = jnp.repeat(jnp.arange(s // GRID_W), GRID_W).astype(F32)
    cols = jnp.tile(jnp.arange(GRID_W), s // GRID_W).astype(F32)
    n_freq = SWA_HEAD_DIM // 4
    inv = ROPE_BASE ** (-jnp.arange(n_freq, dtype=F32) / n_freq)
    ang = jnp.concatenate([rows[:, None] * inv[None, :], cols[:, None] * inv[None, :]], axis=-1)
    cos, sin = jnp.cos(ang), jnp.sin(ang)
    reps = LANE // SWA_HEAD_DIM
    cos_t = jnp.tile(jnp.concatenate([cos, cos], axis=-1), (1, reps))
    sin_t = jnp.tile(jnp.concatenate([-sin, sin], axis=-1), (1, reps))
    cos_t = jnp.concatenate([jnp.ones((lc, LANE), F32), cos_t], axis=0)
    sin_t = jnp.concatenate([jnp.zeros((lc, LANE), F32), sin_t], axis=0)
    return cos_t, sin_t


def _window_attention_mixer(h, mod, gain, w_qkv, q_gain, k_gain, sink, w_out, lc):
    n_batch, t, d = h.shape
    n_heads = sink.shape[0]
    q_w = n_heads * SWA_HEAD_DIM
    kv_w = (w_qkv.shape[1] - q_w) // 2
    cos_t, sin_t = _swa_tables(lc, t - lc)
    w_bf = w_qkv.astype(BF16)
    q = _proj(h, mod, gain, w_bf[:, :q_w], lc)
    k = _proj(h, mod, gain, w_bf[:, q_w:q_w + kv_w], lc)
    v = _proj(h, mod, gain, w_bf[:, q_w + kv_w:], lc)
    q = _headnorm_rope(q, q_gain, cos_t, sin_t, SWA_HEAD_DIM ** -0.5, lc)
    k = _headnorm_rope(k, k_gain, cos_t, sin_t, 1.0, lc)
    y = _attention(q, k, v, sink, lc)
    return _mixer_out(h, y, mod, w_out.astype(BF16), lc)


def _rope_halves(x, cos, sin):
    half = x.shape[1] // 2
    x1, x2 = x[:, :half], x[:, half:]
    return jnp.concatenate([x1 * cos - x2 * sin, x1 * sin + x2 * cos], axis=1)


def _ret_kernel(qf_ref, kf_ref, vf_ref, cosf_ref, sinf_ref, qb_ref, kb_ref, vb_ref, cosb_ref, sinb_ref,
                of_ref, ob_ref, sf_ref, sb_ref):
    c = RET_CHUNK
    rows = qf_ref.shape[0]
    dk = qf_ref.shape[1]
    head = pl.program_id(1)

    @pl.when(pl.program_id(2) == 0)
    def _():
        sf_ref[...] = jnp.zeros_like(sf_ref)
        sb_ref[...] = jnp.zeros_like(sb_ref)

    hv = jnp.zeros((1, 1), F32) + head.astype(F32)
    lg = jnp.log(1.0 - jnp.exp((-5.0 - hv) * math.log(2.0)))
    ri = lax.broadcasted_iota(jnp.int32, (c, c), 0)
    ci = lax.broadcasted_iota(jnp.int32, (c, c), 1)
    pos = lax.broadcasted_iota(jnp.int32, (c, 1), 0).astype(F32)
    chunk_decay = jnp.exp(lg * float(c))

    def run(q_ref, k_ref, v_ref, cos_ref, sin_ref, o_ref, s_ref, reverse):
        rel = ((ci - ri) if reverse else (ri - ci))
        decay = jnp.exp(jnp.where(rel >= 0, lg * rel.astype(F32), NEG_BIG))
        p = (c - 1.0 - pos) if reverse else pos
        q_fac = jnp.exp(lg * (p + 1.0))
        k_fac = jnp.exp(lg * (c - 1.0 - p))
        n_sub = rows // c
        order = range(n_sub - 1, -1, -1) if reverse else range(n_sub)
        for i in order:
            sl = slice(i * c, (i + 1) * c)
            q = _rope_halves(q_ref[sl, :], cos_ref[sl, :], sin_ref[sl, :])
            k = _rope_halves(k_ref[sl, :], cos_ref[sl, :], sin_ref[sl, :]) * (dk ** -0.5)
            v = v_ref[sl, :].astype(BF16)
            scores = _dot_nt(q.astype(BF16), k.astype(BF16)) * decay
            state = s_ref[...]
            o_ref[sl, :] = _dot(scores.astype(BF16), v) + _dot((q * q_fac).astype(BF16), state.astype(BF16))
            s_ref[...] = state * chunk_decay + _dot_tn((k * k_fac).astype(BF16), v)

    run(qf_ref, kf_ref, vf_ref, cosf_ref, sinf_ref, of_ref, sf_ref, False)
    run(qb_ref, kb_ref, vb_ref, cosb_ref, sinb_ref, ob_ref, sb_ref, True)


def _retention_core(proj, cos_t, sin_t, lc):
    n_batch, t, width = proj.shape
    nh = RET_HEADS
    dk = width // (8 * nh)
    dv = 2 * dk
    rows = lc
    n_steps = t // rows
    n_ctx = lc // rows

    def fwd(b, h, s):
        return s

    def bwd(b, h, s):
        return jnp.where(s < n_ctx, n_ctx - 1 - s, n_steps - 1 - s + n_ctx)

    def specs(step):
        return [
            pl.BlockSpec((None, rows, dk), lambda b, h, s: (b, step(b, h, s), h)),
            pl.BlockSpec((None, rows, dk), lambda b, h, s: (b, step(b, h, s), nh + h)),
            pl.BlockSpec((None, rows, dv), lambda b, h, s: (b, step(b, h, s), nh + h)),
            pl.BlockSpec((rows, dk // 2), lambda b, h, s: (step(b, h, s), 0)),
            pl.BlockSpec((rows, dk // 2), lambda b, h, s: (step(b, h, s), 0)),
        ]

    out_sds = jax.ShapeDtypeStruct((n_batch, t, nh * dv), F32)
    return pl.pallas_call(
        _ret_kernel,
        grid=(n_batch, nh, n_steps),
        in_specs=specs(fwd) + specs(bwd),
        out_specs=[
            pl.BlockSpec((None, rows, dv), lambda b, h, s: (b, fwd(b, h, s), h)),
            pl.BlockSpec((None, rows, dv), lambda b, h, s: (b, bwd(b, h, s), h)),
        ],
        out_shape=[out_sds, out_sds],
        scratch_shapes=[pltpu.VMEM((dk, dv), F32), pltpu.VMEM((dk, dv), F32)],
        compiler_params=_params("parallel", "parallel", "arbitrary"),
        name="retention",
    )(proj, proj, proj, cos_t, sin_t, proj, proj, proj, cos_t, sin_t)


def _ret_out_kernel(x_ref, of_ref, ob_ref, gf_ref, gb_ref, gain_ref, mod_ref, w_ref, o_ref, *, n_heads):
    dv = of_ref.shape[1] // n_heads
    parts = []
    for hh in range(n_heads):
        sl = slice(hh * dv, (hh + 1) * dv)

        def group_norm(o, gain):
            mu = jnp.mean(o, axis=-1, keepdims=True)
            xc = o - mu
            var = jnp.mean(xc * xc, axis=-1, keepdims=True)
            return xc * lax.rsqrt(var + EPS) * gain

        y = (_silu(gf_ref[:, sl]) * group_norm(of_ref[:, sl], gain_ref[0:1, sl])
             + _silu(gb_ref[:, sl]) * group_norm(ob_ref[:, sl], gain_ref[1:2, sl]))
        parts.append(y.astype(BF16))
    y = _dot(jnp.concatenate(parts, axis=1), w_ref[...])
    o_ref[...] = x_ref[...] + mod_ref[2:3, :] * y


def _retention_out(h, o_f, o_b, proj, gn_gain, mod, w, lc):
    n_batch, t, d = h.shape
    vw = o_f.shape[-1]
    tm = _row_tile(lc, t - lc)
    gate_f = proj.shape[-1] // vw - 2
    return pl.pallas_call(
        functools.partial(_ret_out_kernel, n_heads=RET_HEADS),
        grid=(n_batch, t // tm),
        in_specs=[
            pl.BlockSpec((None, tm, d), lambda b, i: (b, i, 0)),
            pl.BlockSpec((None, tm, vw), lambda b, i: (b, i, 0)),
            pl.BlockSpec((None, tm, vw), lambda b, i: (b, i, 0)),
            pl.BlockSpec((None, tm, vw), lambda b, i: (b, i, gate_f)),
            pl.BlockSpec((None, tm, vw), lambda b, i: (b, i, gate_f + 1)),
            pl.BlockSpec((2, vw), lambda b, i: (0, 0)),
            pl.BlockSpec((None, 3, d), _mod_index(n_batch, lc // tm)),
            _resident((vw, d), lambda b, i: (0, 0)),
        ],
        out_specs=pl.BlockSpec((None, tm, d), lambda b, i: (b, i, 0)),
        out_shape=jax.ShapeDtypeStruct(h.shape, F32),
        compiler_params=_params("parallel", "parallel"),
        name="retention_out",
    )(h, o_f, o_b, proj, proj, gn_gain, mod, w)


def _ret_tables(lc, s, n_freq):
    inv = ROPE_BASE ** (-jnp.arange(n_freq, dtype=F32) / n_freq)
    ang = jnp.arange(s, dtype=F32)[:, None] * inv[None, :]
    cos_t = jnp.concatenate([jnp.ones((lc, n_freq), F32), jnp.cos(ang)], axis=0)
    sin_t = jnp.concatenate([jnp.zeros((lc, n_freq), F32), jnp.sin(ang)], axis=0)
    return cos_t, sin_t


def _retention_mixer(h, mod, gain, w_in, gn_gain, w_out, lc):
    n_batch, t, d = h.shape
    dk = w_in.shape[1] // (8 * RET_HEADS)
    cos_t, sin_t = _ret_tables(lc, t - lc, dk // 2)
    proj = _proj(h, mod, gain, w_in.astype(BF16), lc)
    o_f, o_b = _retention_core(proj, cos_t, sin_t, lc)
    return _retention_out(h, o_f, o_b, proj, gn_gain, mod, w_out.astype(BF16), lc)


def kernel(x, c, ctx, c_ctx, ada_w, ada_b, norm_g, ffn_w1, ffn_w2, dn_w_in, dn_conv, dn_a_log, dn_dt_bias,
           dn_o_gain, dn_w_out, swa_w_qkv, swa_q_gain, swa_k_gain, swa_sink, swa_w_out, ret_w_in, ret_gn_gain,
           ret_w_out):
    n_batch, s, d = x.shape
    lc = ctx.shape[1]
    depth = ada_w.shape[0]
    n_mod = ada_w.shape[2] // d

    bp = -(-(n_batch + 1) // SUBLANE) * SUBLANE
    cond = jnp.concatenate([c, c_ctx[None, :], jnp.zeros((bp - n_batch - 1, d), F32)], axis=0)
    mods = _ada_mods(cond, ada_w, ada_b).reshape(depth, bp, n_mod, d)

    h = jnp.concatenate([ctx, x], axis=1)
    w1 = ffn_w1.astype(BF16)
    w2 = ffn_w2.astype(BF16)
    for i in range(depth):
        kind, slot = i % N_MIXERS, i // N_MIXERS
        sub = [mods[i, :, 3 * j:3 * j + 3, :] for j in range(3)]
        h = _ffn(h, sub[0], norm_g[i, 0], w1[i, 0], w2[i, 0], lc)
        if kind == 0:
            h = _deltanet_mixer(h, sub[1], norm_g[i, 1], dn_w_in[slot], dn_conv[slot], dn_a_log[slot],
                                dn_dt_bias[slot], dn_o_gain[slot], dn_w_out[slot], lc)
        elif kind == 1:
            h = _window_attention_mixer(h, sub[1], norm_g[i, 1], swa_w_qkv[slot], swa_q_gain[slot],
                                        swa_k_gain[slot], swa_sink[slot], swa_w_out[slot], lc)
        else:
            h = _retention_mixer(h, sub[1], norm_g[i, 1], ret_w_in[slot], ret_gn_gain[slot], ret_w_out[slot], lc)
        h = _ffn(h, sub[2], norm_g[i, 2], w1[i, 1], w2[i, 1], lc)
    return h[:, lc:, :]
```

```python
import functools
import math

import jax
import jax.numpy as jnp
from jax import lax
from jax.experimental import pallas as pl
from jax.experimental.pallas import tpu as pltpu

F32 = jnp.float32
BF16 = jnp.bfloat16

N_MIXERS = 3
GRID_W = 64
ROPE_BASE = 10000.0
EPS = 1e-6
FFN_RESIDUAL = 0.5
DN_HEAD_DIM = 128
DN_CHUNK = 64
DN_INV_BLOCK = 16
DN_CHUNK_BATCHES = (18, 12, 6, 5, 4, 3, 2, 1)
SWA_HEAD_DIM = 64
SWA_GROUP = 4
WINDOW = 128
ATTN_BLOCK = 128
RET_HEADS = 4
RET_CHUNK = 128

LANE = 128
SUBLANE = 8
VMEM_LIMIT_BYTES = 52 * 1024 * 1024
NEG_BIG = -1e30


def _params(*semantics):
    return pltpu.CompilerParams(dimension_semantics=semantics, vmem_limit_bytes=VMEM_LIMIT_BYTES)


def _resident(block_shape, index_map):
    return pl.BlockSpec(block_shape, index_map, pipeline_mode=pl.Buffered(1))


def _dot(a, b):
    return jnp.dot(a, b, preferred_element_type=F32)


def _dot_nt(a, b):
    return lax.dot_general(a, b, (((1,), (1,)), ((), ())), preferred_element_type=F32)


def _dot_tn(a, b):
    return lax.dot_general(a, b, (((0,), (0,)), ((), ())), preferred_element_type=F32)


def _split2(x):
    hi = x.astype(BF16)
    lo = (x - hi.astype(F32)).astype(BF16)
    return hi, lo


def _split3(x):
    hi = x.astype(BF16)
    r = x - hi.astype(F32)
    mid = r.astype(BF16)
    lo = (r - mid.astype(F32)).astype(BF16)
    return hi, mid, lo


def _dot_x3(a, b):
    ah, al = _split2(a)
    bh, bl = _split2(b)
    return _dot(ah, bh) + (_dot(ah, bl) + _dot(al, bh))


def _sigmoid(x):
    return 1.0 / (1.0 + jnp.exp(-x))


def _silu(x):
    return x * _sigmoid(x)


def _softplus(x):
    return jnp.maximum(x, 0.0) + jnp.log(1.0 + jnp.exp(-jnp.abs(x)))


def _mod_norm(x, gain, shift, scale):
    ms = jnp.mean(x * x, axis=-1, keepdims=True)
    return (x * lax.rsqrt(ms + EPS)) * (gain * (1.0 + scale)) + shift


def _ada_kernel(c_ref, w_ref, b_ref, o_ref):
    o_ref[...] = _dot_x3(_silu(c_ref[...]), w_ref[...]) + b_ref[...]


def _ada_mods(cond, ada_w, ada_b):
    depth, d, nd = ada_w.shape
    bp = cond.shape[0]
    tn = d
    return pl.pallas_call(
        _ada_kernel,
        grid=(depth, nd // tn),
        in_specs=[
            pl.BlockSpec((bp, d), lambda l, n: (0, 0)),
            pl.BlockSpec((None, d, tn), lambda l, n: (l, 0, n)),
            pl.BlockSpec((None, 1, tn), lambda l, n: (l, 0, n)),
        ],
        out_specs=pl.BlockSpec((None, bp, tn), lambda l, n: (l, 0, n)),
        out_shape=jax.ShapeDtypeStruct((depth, bp, nd), F32),
        compiler_params=_params("parallel", "parallel"),
        name="ada_mods",
    )(cond, ada_w, ada_b.reshape(depth, 1, nd))


def _row_tile(lc, s):
    tm = 256
    while lc % tm or s % tm:
        tm //= 2
    return tm


def _mod_index(n_batch, n_ctx_tiles):
    return lambda b, t: (jnp.where(t < n_ctx_tiles, n_batch, b), 0, 0)


def _ffn_kernel(x_ref, mod_ref, g_ref, w1_ref, w2_ref, o_ref, *, n_split):
    x = x_ref[...]
    mod = mod_ref[...]
    hn = _mod_norm(x, g_ref[...], mod[0:1], mod[1:2]).astype(BF16)
    f = w2_ref.shape[0]
    fc = f // n_split
    y = None
    for i in range(n_split):
        a = _dot(hn, w1_ref[:, i * fc:(i + 1) * fc])
        b = _dot(hn, w1_ref[:, f + i * fc:f + (i + 1) * fc])
        part = _dot((_silu(a) * b).astype(BF16), w2_ref[i * fc:(i + 1) * fc, :])
        y = part if y is None else y + part
    o_ref[...] = x + (FFN_RESIDUAL * mod[2:3]) * y


def _ffn(h, mod, gain, w1, w2, lc, latent_only=False):
    n_batch, t, d = h.shape
    f = w2.shape[0]
    tm = _row_tile(lc, t - lc)
    skip = lc // tm if latent_only else 0
    n_tiles = t // tm - skip
    return pl.pallas_call(
        functools.partial(_ffn_kernel, n_split=1),
        grid=(n_batch, n_tiles),
        in_specs=[
            pl.BlockSpec((None, tm, d), lambda b, i: (b, i + skip, 0)),
            pl.BlockSpec((None, 3, d), _mod_index(n_batch, lc // tm - skip)),
            pl.BlockSpec((1, d), lambda b, i: (0, 0)),
            _resident((d, 2 * f), lambda b, i: (0, 0)),
            _resident((f, d), lambda b, i: (0, 0)),
        ],
        out_specs=pl.BlockSpec((None, tm, d), lambda b, i: (b, i, 0)),
        out_shape=jax.ShapeDtypeStruct((n_batch, n_tiles * tm, d), F32),
        compiler_params=_params("parallel", "parallel"),
        name="ffn",
    )(h, mod, gain.reshape(1, d), w1, w2)


def _proj_kernel(x_ref, mod_ref, g_ref, w_ref, o_ref):
    mod = mod_ref[...]
    hn = _mod_norm(x_ref[...], g_ref[...], mod[0:1], mod[1:2]).astype(BF16)
    o_ref[...] = _dot(hn, w_ref[...]).astype(o_ref.dtype)


def _col_tile(n):
    tn = min(n, 2048)
    while n % tn:
        tn -= LANE
    return tn


def _proj(h, mod, gain, w, lc, out_dtype=F32):
    n_batch, t, d = h.shape
    n = w.shape[1]
    tm = _row_tile(lc, t - lc)
    tn = _col_tile(n)
    mod_idx = _mod_index(n_batch, lc // tm)
    return pl.pallas_call(
        _proj_kernel,
        grid=(n // tn, n_batch, t // tm),
        in_specs=[
            pl.BlockSpec((None, tm, d), lambda j, b, i: (b, i, 0)),
            pl.BlockSpec((None, 3, d), lambda j, b, i: mod_idx(b, i)),
            pl.BlockSpec((1, d), lambda j, b, i: (0, 0)),
            pl.BlockSpec((d, tn), lambda j, b, i: (0, j)),
        ],
        out_specs=pl.BlockSpec((None, tm, tn), lambda j, b, i: (b, i, j)),
        out_shape=jax.ShapeDtypeStruct((n_batch, t, n), out_dtype),
        compiler_params=_params("parallel", "parallel", "parallel"),
        name="mixer_in",
    )(h, mod, gain.reshape(1, d), w)


def _out_kernel(x_ref, y_ref, mod_ref, w_ref, o_ref):
    y = _dot(y_ref[...].astype(BF16), w_ref[...])
    o_ref[...] = x_ref[...] + mod_ref[2:3, :] * y


def _mixer_out(h, y, mod, w, lc):
    n_batch, t, d = h.shape
    k = y.shape[-1]
    tm = _row_tile(lc, t - lc)
    return pl.pallas_call(
        _out_kernel,
        grid=(n_batch, t // tm),
        in_specs=[
            pl.BlockSpec((None, tm, d), lambda b, i: (b, i, 0)),
            pl.BlockSpec((None, tm, k), lambda b, i: (b, i, 0)),
            pl.BlockSpec((None, 3, d), _mod_index(n_batch, lc // tm)),
            _resident((k, d), lambda b, i: (0, 0)),
        ],
        out_specs=pl.BlockSpec((None, tm, d), lambda b, i: (b, i, 0)),
        out_shape=jax.ShapeDtypeStruct(h.shape, F32),
        compiler_params=_params("parallel", "parallel"),
        name="mixer_out",
    )(h, y, mod, w)


def _bmm(a, b):
    return jnp.einsum("gik,gkj->gij", a.astype(BF16), b.astype(BF16), preferred_element_type=F32)


def _unit_tri_inverse(a, blk_mask, eye_f):
    c = a.shape[-1]
    d = jnp.where(blk_mask, a, 0.0)
    e = a - d
    p = eye_f - d
    dk = d
    for _ in range(DN_INV_BLOCK.bit_length() - 2):
        dk = _bmm(dk, dk)
        p = _bmm(p, eye_f + dk)
    n = _bmm(p, e)
    x = eye_f - n
    nk = n
    for _ in range((c // DN_INV_BLOCK).bit_length() - 2):
        nk = _bmm(nk, nk)
        x = _bmm(x, eye_f + nk)
    return _bmm(x, p)


def _dn_kernel(q_ref, k_ref, v_ref, z_ref, cwq_ref, cwk_ref, cwv_ref, gcol_ref, grow_ref, og_ref,
               y_ref,
               pad_s, q_s, k_s, v_s, kq_s, b_s, a_s, o_s, *, lc):
    t = q_ref.shape[0]
    c = DN_CHUNK
    nc = t // c
    ncc = lc // c
    rb = _row_tile(lc, t - lc)

    zeros8 = jnp.zeros((SUBLANE, DN_HEAD_DIM), F32)

    def conv_act(x_ref, cw_ref, dst, normalise):
        pad_s[0:SUBLANE, :] = zeros8
        pad_s[SUBLANE:SUBLANE + lc, :] = x_ref[0:lc, :]
        pad_s[SUBLANE + lc:2 * SUBLANE + lc, :] = zeros8
        pad_s[2 * SUBLANE + lc:2 * SUBLANE + t, :] = x_ref[lc:t, :]
        pad_s[2 * SUBLANE + t:3 * SUBLANE + t, :] = zeros8
        cw = cw_ref[...]
        n_tap = cw.shape[0]
        for r0 in range(0, t, rb):
            base = r0 + (SUBLANE if r0 < lc else 2 * SUBLANE)
            acc = None
            for j in range(n_tap):
                lo = base + j - n_tap // 2
                term = pad_s[lo:lo + rb, :] * cw[j:j + 1, :]
                acc = term if acc is None else acc + term
            act = _silu(acc)
            if normalise:
                act = act * lax.rsqrt(jnp.sum(act * act, axis=-1, keepdims=True) + EPS)
            dst[r0:r0 + rb, :] = act

    conv_act(q_ref, cwq_ref, q_s, True)
    conv_act(k_ref, cwk_ref, k_s, True)
    conv_act(v_ref, cwv_ref, v_s, False)

    ri = lax.broadcasted_iota(jnp.int32, (c, c), 0)
    ci = lax.broadcasted_iota(jnp.int32, (c, c), 1)
    eye_f = jnp.where(ri == ci, 1.0, 0.0).astype(F32)
    blk_mask = (ri // DN_INV_BLOCK) == (ci // DN_INV_BLOCK)
    incl = (ci <= ri, ci >= ri)
    strict = (ci < ri, ci > ri)
    q_scale = DN_HEAD_DIM ** -0.5
    g = max(n for n in DN_CHUNK_BATCHES if nc % n == 0)

    def phase1(it, carry):
        ch0 = it * g
        rows = pl.ds(pl.multiple_of(ch0 * c, c), g * c)
        q = q_s[rows, :].reshape(g, c, DN_HEAD_DIM) * q_scale
        k = k_s[rows, :].reshape(g, c, DN_HEAD_DIM)
        v = v_s[rows, :].reshape(g, c, DN_HEAD_DIM)
        gcol = gcol_ref[rows, :].reshape(g, c, 4)
        grow = grow_ref[pl.ds(ch0, g)]
        k_bf = k.astype(BF16)
        for d in range(2):
            gc = gcol[:, :, 2 * d:2 * d + 1]
            beta = gcol[:, :, 2 * d + 1:2 * d + 2]
            decay = jnp.exp(jnp.where(incl[d], gc - grow[:, d:d + 1, :], NEG_BIG))
            kb = k * beta
            kq = jnp.einsum("gic,gjc->gij", jnp.concatenate([kb, q], axis=1).astype(BF16), k_bf,
                            preferred_element_type=F32)
            a = jnp.where(strict[d], kq[:, :c] * decay, 0.0)
            qk = kq[:, c:] * decay
            tinv = _unit_tri_inverse(a, blk_mask, eye_f)
            eg = jnp.exp(gc)
            wu = _bmm(tinv, jnp.concatenate([kb * eg, v * beta], axis=2)).astype(BF16)
            g_tot = gc[:, c - 1:c, :] if d == 0 else gc[:, 0:1, :]
            kd = (k * jnp.exp(g_tot - gc)).astype(BF16)
            kwu = jnp.einsum("gck,gcn->gkn", kd, wu, preferred_element_type=F32)
            qwu = _bmm(qk, wu)
            kq_s[d, pl.ds(ch0, g), 0:DN_HEAD_DIM, :] = kwu[:, :, :DN_HEAD_DIM].astype(BF16)
            kq_s[d, pl.ds(ch0, g), DN_HEAD_DIM:, :] = (q * eg - qwu[:, :, :DN_HEAD_DIM]).astype(BF16)
            b_s[d, pl.ds(ch0, g)] = kwu[:, :, DN_HEAD_DIM:]
            o_s[d, rows, :] = qwu[:, :, DN_HEAD_DIM:].reshape(g * c, DN_HEAD_DIM)
            a_s[d, pl.ds(ch0, g)] = jnp.broadcast_to(jnp.exp(g_tot), (g, SUBLANE, DN_HEAD_DIM))
        return carry

    lax.fori_loop(0, nc // g, phase1, 0)

    def phase2(s, carry):
        chunk_of = (s, jnp.where(s < ncc, ncc - 1 - s, nc - 1 - s + ncc))
        new = []
        for d in range(2):
            state = carry[d]
            ch = chunk_of[d]
            rows = pl.ds(pl.multiple_of(ch * c, c), c)
            r = _dot(kq_s[d, ch], state.astype(BF16))
            o_s[d, rows, :] = o_s[d, rows, :] + r[DN_HEAD_DIM:]
            new.append(state * a_s[d, ch][0:1, :] + (b_s[d, ch] - r[:DN_HEAD_DIM]))
        return tuple(new)

    zero_state = jnp.zeros((DN_HEAD_DIM, DN_HEAD_DIM), F32)
    lax.fori_loop(0, nc, phase2, (zero_state, zero_state))

    og = og_ref[...]
    for r0 in range(0, t, rb):
        o = o_s[0, r0:r0 + rb, :] + o_s[1, r0:r0 + rb, :]
        o = o * lax.rsqrt(jnp.mean(o * o, axis=-1, keepdims=True) + EPS) * og
        y_ref[r0:r0 + rb, :] = o * _silu(z_ref[r0:r0 + rb, :])


def _dn_gates_kernel(ab_ref, alog_ref, dtb_ref, o_ref, *, n_heads):
    t, width = ab_ref.shape
    c = DN_CHUNK
    col = lax.broadcasted_iota(jnp.int32, (1, width), 1)
    is_decay = (col % (2 * n_heads)) < n_heads
    backward = col >= 2 * n_heads
    ri = lax.broadcasted_iota(jnp.int32, (c, c), 0)
    ci = lax.broadcasted_iota(jnp.int32, (c, c), 1)
    tri_f = jnp.where(ci <= ri, 1.0, 0.0).astype(BF16)
    tri_b = jnp.where(ci >= ri, 1.0, 0.0).astype(BF16)
    neg_a = -jnp.exp(alog_ref[...])
    dtb = dtb_ref[...]
    for r0 in range(0, t, c):
        raw = ab_ref[r0:r0 + c, :]
        g = neg_a * _softplus(raw + dtb)
        g1, g2, g3 = _split3(g)
        cum_f = _dot(tri_f, g1) + (_dot(tri_f, g2) + _dot(tri_f, g3))
        cum_b = _dot(tri_b, g1) + (_dot(tri_b, g2) + _dot(tri_b, g3))
        o_ref[r0:r0 + c, :] = jnp.where(is_decay, jnp.where(backward, cum_b, cum_f), _sigmoid(raw))


def _dn_gates(ab, a_log, dt_bias):
    n_batch, t, width = ab.shape
    nh = a_log.shape[-1]

    def param_row(p):
        z = jnp.zeros((nh,), F32)
        row = jnp.concatenate([p[0], z, p[1], z])
        return jnp.pad(row, (0, width - 4 * nh)).reshape(1, width)

    return pl.pallas_call(
        functools.partial(_dn_gates_kernel, n_heads=nh),
        grid=(n_batch,),
        in_specs=[
            pl.BlockSpec((None, t, width), lambda b: (b, 0, 0)),
            pl.BlockSpec((1, width), lambda b: (0, 0)),
            pl.BlockSpec((1, width), lambda b: (0, 0)),
        ],
        out_specs=pl.BlockSpec((None, t, width), lambda b: (b, 0, 0)),
        out_shape=jax.ShapeDtypeStruct(ab.shape, F32),
        compiler_params=_params("parallel"),
        name="deltanet_gates",
    )(ab, param_row(a_log), param_row(dt_bias))


def _dn_core(qkvz, gate_cols, gate_rows, conv_w, o_gain, lc):
    n_batch, t, width = qkvz.shape
    hd = DN_HEAD_DIM
    nh = width // (4 * hd)
    nc = t // DN_CHUNK

    def col_block(offset):
        return pl.BlockSpec((None, t, hd), lambda b, h: (b, 0, offset + h))

    def conv_block(offset):
        return pl.BlockSpec((conv_w.shape[0], hd), lambda b, h: (0, offset + h))

    return pl.pallas_call(
        functools.partial(_dn_kernel, lc=lc),
        grid=(n_batch, nh),
        in_specs=[
            col_block(0), col_block(nh), col_block(2 * nh), col_block(3 * nh),
            conv_block(0), conv_block(nh), conv_block(2 * nh),
            pl.BlockSpec((None, None, t, 4), lambda b, h: (b, h, 0, 0)),
            pl.BlockSpec((None, None, nc, 2, DN_CHUNK), lambda b, h: (b, h, 0, 0, 0)),
            pl.BlockSpec((1, hd), lambda b, h: (0, 0)),
        ],
        out_specs=pl.BlockSpec((None, t, hd), lambda b, h: (b, 0, h)),
        out_shape=jax.ShapeDtypeStruct((n_batch, t, nh * hd), F32),
        scratch_shapes=[
            pltpu.VMEM((t + 3 * SUBLANE, hd), F32),
            pltpu.VMEM((t, hd), F32),
            pltpu.VMEM((t, hd), F32),
            pltpu.VMEM((t, hd), F32),
            pltpu.VMEM((2, nc, hd + DN_CHUNK, hd), BF16),
            pltpu.VMEM((2, nc, hd, hd), F32),
            pltpu.VMEM((2, nc, SUBLANE, hd), F32),
            pltpu.VMEM((2, t, hd), F32),
        ],
        compiler_params=_params("parallel", "parallel"),
        name="deltanet",
    )(qkvz, qkvz, qkvz, qkvz, conv_w, conv_w, conv_w, gate_cols, gate_rows, o_gain.reshape(1, hd))


def _deltanet_mixer(h, mod, gain, w_in, conv_w, a_log, dt_bias, o_gain, w_out, lc):
    n_batch, t, d = h.shape
    nh = a_log.shape[-1]
    width = 4 * nh * DN_HEAD_DIM
    qkvz = _proj(h, mod, gain, w_in[:, :width].astype(BF16), lc)
    w_ab = jnp.pad(w_in[:, width:], ((0, 0), (0, LANE - 4 * nh))).astype(BF16)
    gates = _dn_gates(_proj(h, mod, gain, w_ab, lc), a_log, dt_bias)[:, :, :4 * nh]
    gates = gates.reshape(n_batch, t, 2, 2, nh)
    gate_cols = gates.transpose(0, 4, 1, 2, 3).reshape(n_batch, nh, t, 4)
    gate_rows = (gates[:, :, :, 0, :].transpose(0, 3, 2, 1)
                 .reshape(n_batch, nh, 2, t // DN_CHUNK, DN_CHUNK).transpose(0, 1, 3, 2, 4))
    y = _dn_core(qkvz, gate_cols, gate_rows, conv_w, o_gain, lc)
    return _mixer_out(h, y, mod, w_out.astype(BF16), lc)


def _headnorm_rope_kernel(x_ref, gain_ref, cos_ref, sin_ref, e_ref, et_ref, o_ref, *, scale):
    x = x_ref[...]
    width = x.shape[1]
    sq_hi, sq_lo = _split2(x * x)
    ss = _dot(sq_hi, e_ref[...]) + _dot(sq_lo, e_ref[...])
    inv_hi, inv_lo = _split2(lax.rsqrt(ss * (1.0 / SWA_HEAD_DIM) + EPS))
    inv = _dot(inv_hi, et_ref[...]) + _dot(inv_lo, et_ref[...])
    lane = lax.broadcasted_iota(jnp.int32, (1, LANE), 1)
    first_half = (lane % SWA_HEAD_DIM) < SWA_HEAD_DIM // 2
    gain = gain_ref[...]
    cos = cos_ref[...]
    sin = sin_ref[...]
    for s in range(width // LANE):
        xs = x[:, s * LANE:(s + 1) * LANE] * inv[:, s * LANE:(s + 1) * LANE] * gain
        partner = jnp.where(first_half, pltpu.roll(xs, LANE - SWA_HEAD_DIM // 2, 1),
                            pltpu.roll(xs, SWA_HEAD_DIM // 2, 1))
        o_ref[:, s * LANE:(s + 1) * LANE] = (xs * cos + partner * sin) * scale


def _headnorm_rope(x, gain, cos_t, sin_t, scale, lc):
    n_batch, t, width = x.shape
    tm = _row_tile(lc, t - lc)
    heads = lax.broadcasted_iota(jnp.int32, (width, LANE), 0) // SWA_HEAD_DIM
    e = (heads == lax.broadcasted_iota(jnp.int32, (width, LANE), 1)).astype(BF16)
    gain_t = jnp.tile(gain, LANE // SWA_HEAD_DIM).reshape(1, LANE)
    return pl.pallas_call(
        functools.partial(_headnorm_rope_kernel, scale=scale),
        grid=(n_batch, t // tm),
        in_specs=[
            pl.BlockSpec((None, tm, width), lambda b, i: (b, i, 0)),
            pl.BlockSpec((1, LANE), lambda b, i: (0, 0)),
            pl.BlockSpec((tm, LANE), lambda b, i: (i, 0)),
            pl.BlockSpec((tm, LANE), lambda b, i: (i, 0)),
            pl.BlockSpec((width, LANE), lambda b, i: (0, 0)),
            pl.BlockSpec((LANE, width), lambda b, i: (0, 0)),
        ],
        out_specs=pl.BlockSpec((None, tm, width), lambda b, i: (b, i, 0)),
        out_shape=jax.ShapeDtypeStruct(x.shape, F32),
        compiler_params=_params("parallel", "parallel"),
        name="headnorm_rope",
    )(x, gain_t, cos_t, sin_t, e, e.T)


def _attn_kernel(sink_ref, qt_ref, k_ref, vt_ref, o_ref, *, lc):
    n_kv, n_grp, hd, qb = qt_ref.shape
    t = k_ref.shape[1]
    span = qb + 2 * WINDOW
    cols = n_grp * qb
    start = pl.program_id(1) * qb
    is_latent = start >= lc
    ws = pl.multiple_of(jnp.clip(start - WINDOW, lc, t - span), LANE)
    s_pos = ws + lax.broadcasted_iota(jnp.int32, (span, cols), 0)
    t_pos = start + lax.broadcasted_iota(jnp.int32, (span, cols), 1) % qb
    allowed = jnp.abs(t_pos - s_pos) <= jnp.where(is_latent, WINDOW, -1)
    qt = jnp.stack([jnp.concatenate([qt_ref[g, i] for i in range(n_grp)], axis=1) for g in range(n_kv)], axis=0)
    sink = jnp.stack([jnp.concatenate([jnp.full((1, qb), sink_ref[g * n_grp + i], F32) for i in range(n_grp)], axis=1)
                      for g in range(n_kv)], axis=0)
    s_ctx = jnp.einsum("gkd,gdq->gkq", k_ref[:, 0:lc, :], qt, preferred_element_type=F32)
    s_win = jnp.einsum("gkd,gdq->gkq", k_ref[:, pl.ds(ws, span), :], qt, preferred_element_type=F32)
    s_win = jnp.where(allowed, s_win, NEG_BIG)
    m = jnp.maximum(jnp.maximum(jnp.max(s_ctx, axis=1, keepdims=True), jnp.max(s_win, axis=1, keepdims=True)), sink)
    p_ctx = jnp.exp(s_ctx - m)
    p_win = jnp.exp(s_win - m)
    den = jnp.sum(p_ctx, axis=1, keepdims=True) + jnp.sum(p_win, axis=1, keepdims=True) + jnp.exp(sink - m)
    o = (jnp.einsum("gdk,gkq->gdq", vt_ref[:, :, 0:lc], p_ctx.astype(BF16), preferred_element_type=F32)
         + jnp.einsum("gdk,gkq->gdq", vt_ref[:, :, pl.ds(ws, span)], p_win.astype(BF16), preferred_element_type=F32))
    o = (o / den).astype(o_ref.dtype)
    for g in range(n_kv):
        for i in range(n_grp):
            o_ref[g, i] = o[g, :, i * qb:(i + 1) * qb]


def _attention(q, k, v, sink, lc):
    n_batch, t, qw = q.shape
    hd = SWA_HEAD_DIM
    n_kv = k.shape[-1] // hd
    n_grp = qw // (n_kv * hd)
    qt = q.astype(BF16).reshape(n_batch, t, n_kv, n_grp, hd).transpose(0, 2, 3, 4, 1)
    kg = k.astype(BF16).reshape(n_batch, t, n_kv, hd).transpose(0, 2, 1, 3)
    vt = v.astype(BF16).reshape(n_batch, t, n_kv, hd).transpose(0, 2, 3, 1)
    out = pl.pallas_call(
        functools.partial(_attn_kernel, lc=lc),
        grid=(n_batch, t // ATTN_BLOCK),
        in_specs=[
            pl.BlockSpec(memory_space=pltpu.SMEM),
            pl.BlockSpec((None, n_kv, n_grp, hd, ATTN_BLOCK), lambda b, j: (b, 0, 0, 0, j)),
            pl.BlockSpec((None, n_kv, t, hd), lambda b, j: (b, 0, 0, 0)),
            pl.BlockSpec((None, n_kv, hd, t), lambda b, j: (b, 0, 0, 0)),
        ],
        out_specs=pl.BlockSpec((None, n_kv, n_grp, hd, ATTN_BLOCK), lambda b, j: (b, 0, 0, 0, j)),
        out_shape=jax.ShapeDtypeStruct((n_batch, n_kv, n_grp, hd, t), BF16),
        compiler_params=_params("parallel", "parallel"),
        name="window_attention",
    )(sink, qt, kg, vt)
    return out.transpose(0, 4, 1, 2, 3).reshape(n_batch, t, qw)


def _swa_tables(lc, s):
    rows = jnp.repeat(jnp.arange(s // GRID_W), GRID_W).astype(F32)
    cols = jnp.tile(jnp.arange(GRID_W), s // GRID_W).astype(F32)
    n_freq = SWA_HEAD_DIM // 4
    inv = ROPE_BASE ** (-jnp.arange(n_freq, dtype=F32) / n_freq)
    ang = jnp.concatenate([rows[:, None] * inv[None, :], cols[:, None] * inv[None, :]], axis=-1)
    cos, sin = jnp.cos(ang), jnp.sin(ang)
    reps = LANE // SWA_HEAD_DIM
    cos_t = jnp.tile(jnp.concatenate([cos, cos], axis=-1), (1, reps))
    sin_t = jnp.tile(jnp.concatenate([-sin, sin], axis=-1), (1, reps))
    cos_t = jnp.concatenate([jnp.ones((lc, LANE), F32), cos_t], axis=0)
    sin_t = jnp.concatenate([jnp.zeros((lc, LANE), F32), sin_t], axis=0)
    return cos_t, sin_t


def _window_attention_mixer(h, mod, gain, w_qkv, q_gain, k_gain, sink, w_out, lc):
    n_batch, t, d = h.shape
    n_heads = sink.shape[0]
    q_w = n_heads * SWA_HEAD_DIM
    kv_w = (w_qkv.shape[1] - q_w) // 2
    cos_t, sin_t = _swa_tables(lc, t - lc)
    w_bf = w_qkv.astype(BF16)
    q = _proj(h, mod, gain, w_bf[:, :q_w], lc)
    k = _proj(h, mod, gain, w_bf[:, q_w:q_w + kv_w], lc)
    v = _proj(h, mod, gain, w_bf[:, q_w + kv_w:], lc)
    q = _headnorm_rope(q, q_gain, cos_t, sin_t, SWA_HEAD_DIM ** -0.5, lc)
    k = _headnorm_rope(k, k_gain, cos_t, sin_t, 1.0, lc)
    y = _attention(q, k, v, sink, lc)
    return _mixer_out(h, y, mod, w_out.astype(BF16), lc)


def _rope_halves(x, cos, sin):
    half = x.shape[1] // 2
    x1, x2 = x[:, :half], x[:, half:]
    return jnp.concatenate([x1 * cos - x2 * sin, x1 * sin + x2 * cos], axis=1)


def _ret_kernel(qf_ref, kf_ref, vf_ref, cosf_ref, sinf_ref, qb_ref, kb_ref, vb_ref, cosb_ref, sinb_ref,
                of_ref, ob_ref, sf_ref, sb_ref):
    c = RET_CHUNK
    rows = qf_ref.shape[0]
    dk = qf_ref.shape[1]
    head = pl.program_id(1)

    @pl.when(pl.program_id(2) == 0)
    def _():
        sf_ref[...] = jnp.zeros_like(sf_ref)
        sb_ref[...] = jnp.zeros_like(sb_ref)

    hv = jnp.zeros((1, 1), F32) + head.astype(F32)
    lg = jnp.log(1.0 - jnp.exp((-5.0 - hv) * math.log(2.0)))
    ri = lax.broadcasted_iota(jnp.int32, (c, c), 0)
    ci = lax.broadcasted_iota(jnp.int32, (c, c), 1)
    pos = lax.broadcasted_iota(jnp.int32, (c, 1), 0).astype(F32)
    chunk_decay = jnp.exp(lg * float(c))

    def run(q_ref, k_ref, v_ref, cos_ref, sin_ref, o_ref, s_ref, reverse):
        rel = ((ci - ri) if reverse else (ri - ci))
        decay = jnp.exp(jnp.where(rel >= 0, lg * rel.astype(F32), NEG_BIG))
        p = (c - 1.0 - pos) if reverse else pos
        q_fac = jnp.exp(lg * (p + 1.0))
        k_fac = jnp.exp(lg * (c - 1.0 - p))
        n_sub = rows // c
        order = range(n_sub - 1, -1, -1) if reverse else range(n_sub)
        for i in order:
            sl = slice(i * c, (i + 1) * c)
            q = _rope_halves(q_ref[sl, :].astype(F32), cos_ref[sl, :], sin_ref[sl, :])
            k = _rope_halves(k_ref[sl, :].astype(F32), cos_ref[sl, :], sin_ref[sl, :]) * (dk ** -0.5)
            v = v_ref[sl, :].astype(BF16)
            scores = _dot_nt(q.astype(BF16), k.astype(BF16)) * decay
            state = s_ref[...]
            o = _dot(scores.astype(BF16), v) + _dot((q * q_fac).astype(BF16), state.astype(BF16))
            o_ref[sl, :] = o.astype(o_ref.dtype)
            s_ref[...] = state * chunk_decay + _dot_tn((k * k_fac).astype(BF16), v)

    run(qf_ref, kf_ref, vf_ref, cosf_ref, sinf_ref, of_ref, sf_ref, False)
    run(qb_ref, kb_ref, vb_ref, cosb_ref, sinb_ref, ob_ref, sb_ref, True)


def _retention_core(proj, cos_t, sin_t, lc):
    n_batch, t, width = proj.shape
    nh = RET_HEADS
    dk = width // (8 * nh)
    dv = 2 * dk
    rows = lc
    n_steps = t // rows
    n_ctx = lc // rows

    def fwd(b, h, s):
        return s

    def bwd(b, h, s):
        return jnp.where(s < n_ctx, n_ctx - 1 - s, n_steps - 1 - s + n_ctx)

    def specs(step):
        return [
            pl.BlockSpec((None, rows, dk), lambda b, h, s: (b, step(b, h, s), h)),
            pl.BlockSpec((None, rows, dk), lambda b, h, s: (b, step(b, h, s), nh + h)),
            pl.BlockSpec((None, rows, dv), lambda b, h, s: (b, step(b, h, s), nh + h)),
            pl.BlockSpec((rows, dk // 2), lambda b, h, s: (step(b, h, s), 0)),
            pl.BlockSpec((rows, dk // 2), lambda b, h, s: (step(b, h, s), 0)),
        ]

    out_sds = jax.ShapeDtypeStruct((n_batch, t, nh * dv), BF16)
    return pl.pallas_call(
        _ret_kernel,
        grid=(n_batch, nh, n_steps),
        in_specs=specs(fwd) + specs(bwd),
        out_specs=[
            pl.BlockSpec((None, rows, dv), lambda b, h, s: (b, fwd(b, h, s), h)),
            pl.BlockSpec((None, rows, dv), lambda b, h, s: (b, bwd(b, h, s), h)),
        ],
        out_shape=[out_sds, out_sds],
        scratch_shapes=[pltpu.VMEM((dk, dv), F32), pltpu.VMEM((dk, dv), F32)],
        compiler_params=_params("parallel", "parallel", "arbitrary"),
        name="retention",
    )(proj, proj, proj, cos_t, sin_t, proj, proj, proj, cos_t, sin_t)


def _ret_out_kernel(x_ref, of_ref, ob_ref, gf_ref, gb_ref, gain_ref, mod_ref, w_ref, o_ref, *, n_heads):
    dv = of_ref.shape[1] // n_heads
    parts = []
    for hh in range(n_heads):
        sl = slice(hh * dv, (hh + 1) * dv)

        def group_norm(o, gain):
            mu = jnp.mean(o, axis=-1, keepdims=True)
            xc = o - mu
            var = jnp.mean(xc * xc, axis=-1, keepdims=True)
            return xc * lax.rsqrt(var + EPS) * gain

        y = (_silu(gf_ref[:, sl].astype(F32)) * group_norm(of_ref[:, sl].astype(F32), gain_ref[0:1, sl])
             + _silu(gb_ref[:, sl].astype(F32)) * group_norm(ob_ref[:, sl].astype(F32), gain_ref[1:2, sl]))
        parts.append(y.astype(BF16))
    y = _dot(jnp.concatenate(parts, axis=1), w_ref[...])
    o_ref[...] = x_ref[...] + mod_ref[2:3, :] * y


def _retention_out(h, o_f, o_b, proj, gn_gain, mod, w, lc):
    n_batch, t, d = h.shape
    vw = o_f.shape[-1]
    tm = _row_tile(lc, t - lc)
    gate_f = proj.shape[-1] // vw - 2
    return pl.pallas_call(
        functools.partial(_ret_out_kernel, n_heads=RET_HEADS),
        grid=(n_batch, t // tm),
        in_specs=[
            pl.BlockSpec((None, tm, d), lambda b, i: (b, i, 0)),
            pl.BlockSpec((None, tm, vw), lambda b, i: (b, i, 0)),
            pl.BlockSpec((None, tm, vw), lambda b, i: (b, i, 0)),
            pl.BlockSpec((None, tm, vw), lambda b, i: (b, i, gate_f)),
            pl.BlockSpec((None, tm, vw), lambda b, i: (b, i, gate_f + 1)),
            pl.BlockSpec((2, vw), lambda b, i: (0, 0)),
            pl.BlockSpec((None, 3, d), _mod_index(n_batch, lc // tm)),
            _resident((vw, d), lambda b, i: (0, 0)),
        ],
        out_specs=pl.BlockSpec((None, tm, d), lambda b, i: (b, i, 0)),
        out_shape=jax.ShapeDtypeStruct(h.shape, F32),
        compiler_params=_params("parallel", "parallel"),
        name="retention_out",
    )(h, o_f, o_b, proj, proj, gn_gain, mod, w)


def _ret_tables(lc, s, n_freq):
    inv = ROPE_BASE ** (-jnp.arange(n_freq, dtype=F32) / n_freq)
    ang = jnp.arange(s, dtype=F32)[:, None] * inv[None, :]
    cos_t = jnp.concatenate([jnp.ones((lc, n_freq), F32), jnp.cos(ang)], axis=0)
    sin_t = jnp.concatenate([jnp.zeros((lc, n_freq), F32), jnp.sin(ang)], axis=0)
    return cos_t, sin_t


def _retention_mixer(h, mod, gain, w_in, gn_gain, w_out, lc):
    n_batch, t, d = h.shape
    dk = w_in.shape[1] // (8 * RET_HEADS)
    cos_t, sin_t = _ret_tables(lc, t - lc, dk // 2)
    proj = _proj(h, mod, gain, w_in.astype(BF16), lc, out_dtype=BF16)
    o_f, o_b = _retention_core(proj, cos_t, sin_t, lc)
    return _retention_out(h, o_f, o_b, proj, gn_gain, mod, w_out.astype(BF16), lc)


def kernel(x, c, ctx, c_ctx, ada_w, ada_b, norm_g, ffn_w1, ffn_w2, dn_w_in, dn_conv, dn_a_log, dn_dt_bias,
           dn_o_gain, dn_w_out, swa_w_qkv, swa_q_gain, swa_k_gain, swa_sink, swa_w_out, ret_w_in, ret_gn_gain,
           ret_w_out):
    n_batch, s, d = x.shape
    lc = ctx.shape[1]
    depth = ada_w.shape[0]
    n_mod = ada_w.shape[2] // d

    bp = -(-(n_batch + 1) // SUBLANE) * SUBLANE
    cond = jnp.concatenate([c, c_ctx[None, :], jnp.zeros((bp - n_batch - 1, d), F32)], axis=0)
    mods = _ada_mods(cond, ada_w, ada_b).reshape(depth, bp, n_mod, d)

    h = jnp.concatenate([ctx, x], axis=1)
    w1 = ffn_w1.astype(BF16)
    w2 = ffn_w2.astype(BF16)
    for i in range(depth):
        kind, slot = i % N_MIXERS, i // N_MIXERS
        sub = [mods[i, :, 3 * j:3 * j + 3, :] for j in range(3)]
        h = _ffn(h, sub[0], norm_g[i, 0], w1[i, 0], w2[i, 0], lc)
        if kind == 0:
            h = _deltanet_mixer(h, sub[1], norm_g[i, 1], dn_w_in[slot], dn_conv[slot], dn_a_log[slot],
                                dn_dt_bias[slot], dn_o_gain[slot], dn_w_out[slot], lc)
        elif kind == 1:
            h = _window_attention_mixer(h, sub[1], norm_g[i, 1], swa_w_qkv[slot], swa_q_gain[slot],
                                        swa_k_gain[slot], swa_sink[slot], swa_w_out[slot], lc)
        else:
            h = _retention_mixer(h, sub[1], norm_g[i, 1], ret_w_in[slot], ret_gn_gain[slot], ret_w_out[slot], lc)
        h = _ffn(h, sub[2], norm_g[i, 2], w1[i, 1], w2[i, 1], lc, latent_only=(i == depth - 1))
    return h
```

```python
import functools
import math

import jax
import jax.numpy as jnp
from jax import lax
from jax.experimental import pallas as pl
from jax.experimental.pallas import tpu as pltpu

F32 = jnp.float32
BF16 = jnp.bfloat16

N_MIXERS = 3
GRID_W = 64
ROPE_BASE = 10000.0
EPS = 1e-6
FFN_RESIDUAL = 0.5
DN_HEAD_DIM = 128
DN_CHUNK = 64
DN_INV_BLOCK = 16
DN_CHUNK_BATCHES = (18, 12, 6, 5, 4, 3, 2, 1)
SWA_HEAD_DIM = 64
SWA_GROUP = 4
WINDOW = 128
ATTN_BLOCK = 128
RET_HEADS = 4
RET_CHUNK = 128

LANE = 128
SUBLANE = 8
VMEM_LIMIT_BYTES = 52 * 1024 * 1024
NEG_BIG = -1e30


def _params(*semantics):
    return pltpu.CompilerParams(dimension_semantics=semantics, vmem_limit_bytes=VMEM_LIMIT_BYTES)


def _resident(block_shape, index_map):
    return pl.BlockSpec(block_shape, index_map, pipeline_mode=pl.Buffered(1))


def _dot(a, b):
    return jnp.dot(a, b, preferred_element_type=F32)


def _dot_nt(a, b):
    return lax.dot_general(a, b, (((1,), (1,)), ((), ())), preferred_element_type=F32)


def _dot_tn(a, b):
    return lax.dot_general(a, b, (((0,), (0,)), ((), ())), preferred_element_type=F32)


def _split2(x):
    hi = x.astype(BF16)
    lo = (x - hi.astype(F32)).astype(BF16)
    return hi, lo


def _split3(x):
    hi = x.astype(BF16)
    r = x - hi.astype(F32)
    mid = r.astype(BF16)
    lo = (r - mid.astype(F32)).astype(BF16)
    return hi, mid, lo


def _dot_x3(a, b):
    ah, al = _split2(a)
    bh, bl = _split2(b)
    return _dot(ah, bh) + (_dot(ah, bl) + _dot(al, bh))


def _sigmoid(x):
    return 1.0 / (1.0 + jnp.exp(-x))


def _silu(x):
    return x * _sigmoid(x)


def _softplus(x):
    return jnp.maximum(x, 0.0) + jnp.log(1.0 + jnp.exp(-jnp.abs(x)))


def _mod_norm(x, gain, shift, scale):
    ms = jnp.mean(x * x, axis=-1, keepdims=True)
    return (x * lax.rsqrt(ms + EPS)) * (gain * (1.0 + scale)) + shift


def _ada_kernel(c_ref, w_ref, b_ref, o_ref):
    o_ref[...] = _dot_x3(_silu(c_ref[...]), w_ref[...]) + b_ref[...]


def _ada_mods(cond, ada_w, ada_b):
    depth, d, nd = ada_w.shape
    bp = cond.shape[0]
    tn = d
    return pl.pallas_call(
        _ada_kernel,
        grid=(depth, nd // tn),
        in_specs=[
            pl.BlockSpec((bp, d), lambda l, n: (0, 0)),
            pl.BlockSpec((None, d, tn), lambda l, n: (l, 0, n)),
            pl.BlockSpec((None, 1, tn), lambda l, n: (l, 0, n)),
        ],
        out_specs=pl.BlockSpec((None, bp, tn), lambda l, n: (l, 0, n)),
        out_shape=jax.ShapeDtypeStruct((depth, bp, nd), F32),
        compiler_params=_params("parallel", "parallel"),
        name="ada_mods",
    )(cond, ada_w, ada_b.reshape(depth, 1, nd))


def _row_tile(lc, s):
    tm = 256
    while lc % tm or s % tm:
        tm //= 2
    return tm


def _mod_index(n_batch, n_ctx_tiles):
    return lambda b, t: (jnp.where(t < n_ctx_tiles, n_batch, b), 0, 0)


def _ffn_rows(x, mod_ref, g_ref, w1_ref, w2_ref, o_ref):
    mod = mod_ref[...]
    hn = _mod_norm(x, g_ref[...], mod[0:1], mod[1:2]).astype(BF16)
    f = w2_ref.shape[0]
    a = _dot(hn, w1_ref[:, :f])
    b = _dot(hn, w1_ref[:, f:])
    y = _dot((_silu(a) * b).astype(BF16), w2_ref[...])
    o_ref[...] = x + (FFN_RESIDUAL * mod[2:3]) * y


def _ffn_kernel(x_ref, mod_ref, g_ref, w1_ref, w2_ref, o_ref):
    _ffn_rows(x_ref[...], mod_ref, g_ref, w1_ref, w2_ref, o_ref)


def _ffn_joining_kernel(ctx_ref, x_ref, mod_ref, g_ref, w1_ref, w2_ref, o_ref, *, n_ctx_tiles):
    tile = pl.program_id(1) + jnp.zeros(x_ref.shape, jnp.int32)
    _ffn_rows(jnp.where(tile < n_ctx_tiles, ctx_ref[...], x_ref[...]), mod_ref, g_ref, w1_ref, w2_ref, o_ref)


def _ffn(h, mod, gain, w1, w2, lc, latent_only=False):
    n_batch, t, d = h.shape
    f = w2.shape[0]
    tm = _row_tile(lc, t - lc)
    skip = lc // tm if latent_only else 0
    n_tiles = t // tm - skip
    return pl.pallas_call(
        _ffn_kernel,
        grid=(n_batch, n_tiles),
        in_specs=[
            pl.BlockSpec((None, tm, d), lambda b, i: (b, i + skip, 0)),
            pl.BlockSpec((None, 3, d), _mod_index(n_batch, lc // tm - skip)),
            pl.BlockSpec((1, d), lambda b, i: (0, 0)),
            _resident((d, 2 * f), lambda b, i: (0, 0)),
            _resident((f, d), lambda b, i: (0, 0)),
        ],
        out_specs=pl.BlockSpec((None, tm, d), lambda b, i: (b, i, 0)),
        out_shape=jax.ShapeDtypeStruct((n_batch, n_tiles * tm, d), F32),
        compiler_params=_params("parallel", "parallel"),
        name="ffn",
    )(h, mod, gain.reshape(1, d), w1, w2)


def _ffn_joining(ctx, x, mod, gain, w1, w2):
    n_batch, s, d = x.shape
    lc = ctx.shape[1]
    f = w2.shape[0]
    tm = _row_tile(lc, s)
    nct = lc // tm
    return pl.pallas_call(
        functools.partial(_ffn_joining_kernel, n_ctx_tiles=nct),
        grid=(n_batch, (lc + s) // tm),
        in_specs=[
            pl.BlockSpec((None, tm, d), lambda b, i: (b, jnp.minimum(i, nct - 1), 0)),
            pl.BlockSpec((None, tm, d), lambda b, i: (b, jnp.maximum(i - nct, 0), 0)),
            pl.BlockSpec((None, 3, d), _mod_index(n_batch, nct)),
            pl.BlockSpec((1, d), lambda b, i: (0, 0)),
            _resident((d, 2 * f), lambda b, i: (0, 0)),
            _resident((f, d), lambda b, i: (0, 0)),
        ],
        out_specs=pl.BlockSpec((None, tm, d), lambda b, i: (b, i, 0)),
        out_shape=jax.ShapeDtypeStruct((n_batch, lc + s, d), F32),
        compiler_params=_params("parallel", "parallel"),
        name="ffn_joining",
    )(ctx, x, mod, gain.reshape(1, d), w1, w2)


def _proj_kernel(x_ref, mod_ref, g_ref, w_ref, o_ref):
    mod = mod_ref[...]
    hn = _mod_norm(x_ref[...], g_ref[...], mod[0:1], mod[1:2]).astype(BF16)
    o_ref[...] = _dot(hn, w_ref[...]).astype(o_ref.dtype)


PROJ_WEIGHT_TILE_BYTES = 16 * 1024 * 1024


def _col_tile(w):
    d, n = w.shape
    if w.size * w.dtype.itemsize <= PROJ_WEIGHT_TILE_BYTES:
        return n
    tn = 2048
    while n % tn:
        tn -= LANE
    return tn


def _proj(h, mod, gain, w, lc, out_dtype=F32):
    n_batch, t, d = h.shape
    n = w.shape[1]
    tm = _row_tile(lc, t - lc)
    tn = _col_tile(w)
    mod_idx = _mod_index(n_batch, lc // tm)
    w_spec = (_resident if tn == n else pl.BlockSpec)((d, tn), lambda j, b, i: (0, j))
    return pl.pallas_call(
        _proj_kernel,
        grid=(n // tn, n_batch, t // tm),
        in_specs=[
            pl.BlockSpec((None, tm, d), lambda j, b, i: (b, i, 0)),
            pl.BlockSpec((None, 3, d), lambda j, b, i: mod_idx(b, i)),
            pl.BlockSpec((1, d), lambda j, b, i: (0, 0)),
            w_spec,
        ],
        out_specs=pl.BlockSpec((None, tm, tn), lambda j, b, i: (b, i, j)),
        out_shape=jax.ShapeDtypeStruct((n_batch, t, n), out_dtype),
        compiler_params=_params("parallel", "parallel", "parallel"),
        name="mixer_in",
    )(h, mod, gain.reshape(1, d), w)


def _out_kernel(x_ref, y_ref, mod_ref, w_ref, o_ref):
    y = _dot(y_ref[...].astype(BF16), w_ref[...])
    o_ref[...] = x_ref[...] + mod_ref[2:3, :] * y


def _mixer_out(h, y, mod, w, lc):
    n_batch, t, d = h.shape
    k = y.shape[-1]
    tm = _row_tile(lc, t - lc)
    return pl.pallas_call(
        _out_kernel,
        grid=(n_batch, t // tm),
        in_specs=[
            pl.BlockSpec((None, tm, d), lambda b, i: (b, i, 0)),
            pl.BlockSpec((None, tm, k), lambda b, i: (b, i, 0)),
            pl.BlockSpec((None, 3, d), _mod_index(n_batch, lc // tm)),
            _resident((k, d), lambda b, i: (0, 0)),
        ],
        out_specs=pl.BlockSpec((None, tm, d), lambda b, i: (b, i, 0)),
        out_shape=jax.ShapeDtypeStruct(h.shape, F32),
        compiler_params=_params("parallel", "parallel"),
        name="mixer_out",
    )(h, y, mod, w)


def _bmm(a, b):
    return jnp.einsum("gik,gkj->gij", a.astype(BF16), b.astype(BF16), preferred_element_type=F32)


def _pair_mm(x, y, left):
    y = y.astype(BF16)
    zero = jnp.zeros_like(y)
    y_diag = jnp.concatenate([jnp.where(left, y, zero), jnp.where(left, zero, y)], axis=1)
    return jnp.einsum("gik,gkj->gij", x.astype(BF16), y_diag, preferred_element_type=F32)


def _unit_tri_inverse(a, blk_mask, eye_f, left):
    c = a.shape[1]
    d = jnp.where(blk_mask, a, 0.0)
    e = a - d
    p = eye_f - d
    dk = d
    for _ in range(DN_INV_BLOCK.bit_length() - 2):
        dk = _pair_mm(dk, dk, left)
        p = _pair_mm(p, eye_f + dk, left)
    n = _pair_mm(p, e, left)
    x = eye_f - n
    nk = n
    for _ in range((c // DN_INV_BLOCK).bit_length() - 2):
        nk = _pair_mm(nk, nk, left)
        x = _pair_mm(x, eye_f + nk, left)
    return _pair_mm(x, p, left)


def _dn_kernel(q_ref, k_ref, v_ref, z_ref, cwq_ref, cwk_ref, cwv_ref, gcol_ref, grow_ref, og_ref,
               y_ref,
               pad_s, q_s, k_s, v_s, kq_s, b_s, a_s, o_s, *, lc):
    t = q_ref.shape[0]
    c = DN_CHUNK
    nc = t // c
    ncc = lc // c
    rb = _row_tile(lc, t - lc)

    zeros8 = jnp.zeros((SUBLANE, DN_HEAD_DIM), F32)

    def conv_act(x_ref, cw_ref, dst, normalise):
        pad_s[0:SUBLANE, :] = zeros8
        pad_s[SUBLANE:SUBLANE + lc, :] = x_ref[0:lc, :]
        pad_s[SUBLANE + lc:2 * SUBLANE + lc, :] = zeros8
        pad_s[2 * SUBLANE + lc:2 * SUBLANE + t, :] = x_ref[lc:t, :]
        pad_s[2 * SUBLANE + t:3 * SUBLANE + t, :] = zeros8
        cw = cw_ref[...]
        n_tap = cw.shape[0]
        for r0 in range(0, t, rb):
            base = r0 + (SUBLANE if r0 < lc else 2 * SUBLANE)
            acc = None
            for j in range(n_tap):
                lo = base + j - n_tap // 2
                term = pad_s[lo:lo + rb, :] * cw[j:j + 1, :]
                acc = term if acc is None else acc + term
            act = _silu(acc)
            if normalise:
                act = act * lax.rsqrt(jnp.sum(act * act, axis=-1, keepdims=True) + EPS)
            dst[r0:r0 + rb, :] = act

    conv_act(q_ref, cwq_ref, q_s, True)
    conv_act(k_ref, cwk_ref, k_s, True)
    conv_act(v_ref, cwv_ref, v_s, False)

    ri = lax.broadcasted_iota(jnp.int32, (c, 2 * c), 0)
    li = lax.broadcasted_iota(jnp.int32, (c, 2 * c), 1)
    left = li < c
    ci = jnp.where(left, li, li - c)
    eye_f = jnp.where(ri == ci, 1.0, 0.0).astype(F32)
    blk_mask = (ri // DN_INV_BLOCK) == (ci // DN_INV_BLOCK)
    lag = jnp.where(left, ri - ci, ci - ri)
    incl = lag >= 0
    strict = lag > 0
    q_scale = DN_HEAD_DIM ** -0.5
    g = max(n for n in DN_CHUNK_BATCHES if nc % n == 0)

    def phase1(it, carry):
        ch0 = it * g
        rows = pl.ds(pl.multiple_of(ch0 * c, c), g * c)
        q = q_s[rows, :].reshape(g, c, DN_HEAD_DIM) * q_scale
        k = k_s[rows, :].reshape(g, c, DN_HEAD_DIM)
        v = v_s[rows, :].reshape(g, c, DN_HEAD_DIM)
        gcol = gcol_ref[rows, :].reshape(g, c, 4)
        grow = grow_ref[pl.ds(ch0, g)]
        k_bf = k.astype(BF16)
        gcs = (gcol[:, :, 0:1], gcol[:, :, 2:3])
        kbs = (k * gcol[:, :, 1:2], k * gcol[:, :, 3:4])
        decay = jnp.exp(jnp.where(incl, jnp.where(left, gcs[0], gcs[1]) - grow, NEG_BIG))
        kq = jnp.einsum("gic,gjc->gij", jnp.concatenate([kbs[0], kbs[1], q], axis=1).astype(BF16),
                        jnp.concatenate([k_bf, k_bf], axis=1), preferred_element_type=F32)
        a = jnp.where(strict, jnp.where(left, kq[:, :c], kq[:, c:2 * c]) * decay, 0.0)
        qk = (kq[:, 2 * c:] * decay).astype(BF16)
        tinv = _unit_tri_inverse(a, blk_mask, eye_f, left).astype(BF16)
        for d in range(2):
            gc, kb, beta = gcs[d], kbs[d], gcol[:, :, 2 * d + 1:2 * d + 2]

            def own_rows(x):
                zero = jnp.zeros_like(x)
                return jnp.concatenate([x, zero] if d == 0 else [zero, x], axis=1)

            eg = jnp.exp(gc)
            wu = jnp.einsum("gik,gkj->gij", tinv, own_rows(jnp.concatenate([kb * eg, v * beta], axis=2).astype(BF16)),
                            preferred_element_type=F32).astype(BF16)
            g_tot = gc[:, c - 1:c, :] if d == 0 else gc[:, 0:1, :]
            kd = (k * jnp.exp(g_tot - gc)).astype(BF16)
            kwu = jnp.einsum("gck,gcn->gkn", kd, wu, preferred_element_type=F32)
            qwu = jnp.einsum("gik,gkj->gij", qk, own_rows(wu), preferred_element_type=F32)
            kq_s[d, pl.ds(ch0, g), 0:DN_HEAD_DIM, :] = kwu[:, :, :DN_HEAD_DIM].astype(BF16)
            kq_s[d, pl.ds(ch0, g), DN_HEAD_DIM:, :] = (q * eg - qwu[:, :, :DN_HEAD_DIM]).astype(BF16)
            b_s[d, pl.ds(ch0, g)] = kwu[:, :, DN_HEAD_DIM:]
            o_s[d, rows, :] = qwu[:, :, DN_HEAD_DIM:].reshape(g * c, DN_HEAD_DIM)
            a_s[d, pl.ds(ch0, g)] = jnp.broadcast_to(jnp.exp(g_tot), (g, SUBLANE, DN_HEAD_DIM))
        return carry

    lax.fori_loop(0, nc // g, phase1, 0)

    def phase2(s, carry):
        chunk_of = (s, jnp.where(s < ncc, ncc - 1 - s, nc - 1 - s + ncc))
        new = []
        for d in range(2):
            state = carry[d]
            ch = chunk_of[d]
            rows = pl.ds(pl.multiple_of(ch * c, c), c)
            r = _dot(kq_s[d, ch], state.astype(BF16))
            o_s[d, rows, :] = o_s[d, rows, :] + r[DN_HEAD_DIM:]
            new.append(state * a_s[d, ch][0:1, :] + (b_s[d, ch] - r[:DN_HEAD_DIM]))
        return tuple(new)

    zero_state = jnp.zeros((DN_HEAD_DIM, DN_HEAD_DIM), F32)
    lax.fori_loop(0, nc, phase2, (zero_state, zero_state))

    og = og_ref[...]
    for r0 in range(0, t, rb):
        o = o_s[0, r0:r0 + rb, :] + o_s[1, r0:r0 + rb, :]
        o = o * lax.rsqrt(jnp.mean(o * o, axis=-1, keepdims=True) + EPS) * og
        y_ref[r0:r0 + rb, :] = o * _silu(z_ref[r0:r0 + rb, :])


def _dn_gates_kernel(ab_ref, alog_ref, dtb_ref, o_ref, *, n_heads):
    t, width = ab_ref.shape
    c = DN_CHUNK
    col = lax.broadcasted_iota(jnp.int32, (1, width), 1)
    is_decay = (col % (2 * n_heads)) < n_heads
    backward = col >= 2 * n_heads
    ri = lax.broadcasted_iota(jnp.int32, (c, c), 0)
    ci = lax.broadcasted_iota(jnp.int32, (c, c), 1)
    tri_f = jnp.where(ci <= ri, 1.0, 0.0).astype(BF16)
    tri_b = jnp.where(ci >= ri, 1.0, 0.0).astype(BF16)
    neg_a = -jnp.exp(alog_ref[...])
    dtb = dtb_ref[...]
    for r0 in range(0, t, c):
        raw = ab_ref[r0:r0 + c, :]
        g = neg_a * _softplus(raw + dtb)
        g1, g2, g3 = _split3(g)
        cum_f = _dot(tri_f, g1) + (_dot(tri_f, g2) + _dot(tri_f, g3))
        cum_b = _dot(tri_b, g1) + (_dot(tri_b, g2) + _dot(tri_b, g3))
        o_ref[r0:r0 + c, :] = jnp.where(is_decay, jnp.where(backward, cum_b, cum_f), _sigmoid(raw))


def _dn_gates(ab, a_log, dt_bias):
    n_batch, t, width = ab.shape
    nh = a_log.shape[-1]

    def param_row(p):
        z = jnp.zeros((nh,), F32)
        row = jnp.concatenate([p[0], z, p[1], z])
        return jnp.pad(row, (0, width - 4 * nh)).reshape(1, width)

    return pl.pallas_call(
        functools.partial(_dn_gates_kernel, n_heads=nh),
        grid=(n_batch,),
        in_specs=[
            pl.BlockSpec((None, t, width), lambda b: (b, 0, 0)),
            pl.BlockSpec((1, width), lambda b: (0, 0)),
            pl.BlockSpec((1, width), lambda b: (0, 0)),
        ],
        out_specs=pl.BlockSpec((None, t, width), lambda b: (b, 0, 0)),
        out_shape=jax.ShapeDtypeStruct(ab.shape, F32),
        compiler_params=_params("parallel"),
        name="deltanet_gates",
    )(ab, param_row(a_log), param_row(dt_bias))


def _dn_core(qkvz, gate_cols, gate_rows, conv_w, o_gain, lc):
    n_batch, t, width = qkvz.shape
    hd = DN_HEAD_DIM
    nh = width // (4 * hd)
    nc = t // DN_CHUNK

    def col_block(offset):
        return pl.BlockSpec((None, t, hd), lambda b, h: (b, 0, offset + h))

    def conv_block(offset):
        return pl.BlockSpec((conv_w.shape[0], hd), lambda b, h: (0, offset + h))

    return pl.pallas_call(
        functools.partial(_dn_kernel, lc=lc),
        grid=(n_batch, nh),
        in_specs=[
            col_block(0), col_block(nh), col_block(2 * nh), col_block(3 * nh),
            conv_block(0), conv_block(nh), conv_block(2 * nh),
            pl.BlockSpec((None, None, t, 4), lambda b, h: (b, h, 0, 0)),
            pl.BlockSpec((None, None, nc, 1, 2 * DN_CHUNK), lambda b, h: (b, h, 0, 0, 0)),
            pl.BlockSpec((1, hd), lambda b, h: (0, 0)),
        ],
        out_specs=pl.BlockSpec((None, t, hd), lambda b, h: (b, 0, h)),
        out_shape=jax.ShapeDtypeStruct((n_batch, t, nh * hd), F32),
        scratch_shapes=[
            pltpu.VMEM((t + 3 * SUBLANE, hd), F32),
            pltpu.VMEM((t, hd), F32),
            pltpu.VMEM((t, hd), F32),
            pltpu.VMEM((t, hd), F32),
            pltpu.VMEM((2, nc, hd + DN_CHUNK, hd), BF16),
            pltpu.VMEM((2, nc, hd, hd), F32),
            pltpu.VMEM((2, nc, SUBLANE, hd), F32),
            pltpu.VMEM((2, t, hd), F32),
        ],
        compiler_params=_params("parallel", "parallel"),
        name="deltanet",
    )(qkvz, qkvz, qkvz, qkvz, conv_w, conv_w, conv_w, gate_cols, gate_rows, o_gain.reshape(1, hd))


def _deltanet_mixer(h, mod, gain, w_in, conv_w, a_log, dt_bias, o_gain, w_out, lc):
    n_batch, t, d = h.shape
    nh = a_log.shape[-1]
    width = 4 * nh * DN_HEAD_DIM
    qkvz = _proj(h, mod, gain, w_in[:, :width].astype(BF16), lc)
    w_ab = jnp.pad(w_in[:, width:], ((0, 0), (0, LANE - 4 * nh))).astype(BF16)
    gates = _dn_gates(_proj(h, mod, gain, w_ab, lc), a_log, dt_bias)[:, :, :4 * nh]
    gates = gates.reshape(n_batch, t, 2, 2, nh)
    gate_cols = gates.transpose(0, 4, 1, 2, 3).reshape(n_batch, nh, t, 4)
    gate_rows = (gates[:, :, :, 0, :].transpose(0, 3, 2, 1)
                 .reshape(n_batch, nh, 2, t // DN_CHUNK, DN_CHUNK).transpose(0, 1, 3, 2, 4)
                 .reshape(n_batch, nh, t // DN_CHUNK, 1, 2 * DN_CHUNK))
    y = _dn_core(qkvz, gate_cols, gate_rows, conv_w, o_gain, lc)
    return _mixer_out(h, y, mod, w_out.astype(BF16), lc)


def _headnorm_rope_kernel(x_ref, gain_ref, cos_ref, sin_ref, e_ref, et_ref, o_ref, *, scale):
    x = x_ref[...]
    width = x.shape[1]
    sq_hi, sq_lo = _split2(x * x)
    ss = _dot(sq_hi, e_ref[...]) + _dot(sq_lo, e_ref[...])
    inv_hi, inv_lo = _split2(lax.rsqrt(ss * (1.0 / SWA_HEAD_DIM) + EPS))
    inv = _dot(inv_hi, et_ref[...]) + _dot(inv_lo, et_ref[...])
    lane = lax.broadcasted_iota(jnp.int32, (1, LANE), 1)
    first_half = (lane % SWA_HEAD_DIM) < SWA_HEAD_DIM // 2
    gain = gain_ref[...]
    cos = cos_ref[...]
    sin = sin_ref[...]
    for s in range(width // LANE):
        xs = x[:, s * LANE:(s + 1) * LANE] * inv[:, s * LANE:(s + 1) * LANE] * gain
        partner = jnp.where(first_half, pltpu.roll(xs, LANE - SWA_HEAD_DIM // 2, 1),
                            pltpu.roll(xs, SWA_HEAD_DIM // 2, 1))
        o_ref[:, s * LANE:(s + 1) * LANE] = (xs * cos + partner * sin) * scale


def _headnorm_rope(x, gain, cos_t, sin_t, scale, lc):
    n_batch, t, width = x.shape
    tm = _row_tile(lc, t - lc)
    heads = lax.broadcasted_iota(jnp.int32, (width, LANE), 0) // SWA_HEAD_DIM
    e = (heads == lax.broadcasted_iota(jnp.int32, (width, LANE), 1)).astype(BF16)
    gain_t = jnp.tile(gain, LANE // SWA_HEAD_DIM).reshape(1, LANE)
    return pl.pallas_call(
        functools.partial(_headnorm_rope_kernel, scale=scale),
        grid=(n_batch, t // tm),
        in_specs=[
            pl.BlockSpec((None, tm, width), lambda b, i: (b, i, 0)),
            pl.BlockSpec((1, LANE), lambda b, i: (0, 0)),
            pl.BlockSpec((tm, LANE), lambda b, i: (i, 0)),
            pl.BlockSpec((tm, LANE), lambda b, i: (i, 0)),
            pl.BlockSpec((width, LANE), lambda b, i: (0, 0)),
            pl.BlockSpec((LANE, width), lambda b, i: (0, 0)),
        ],
        out_specs=pl.BlockSpec((None, tm, width), lambda b, i: (b, i, 0)),
        out_shape=jax.ShapeDtypeStruct(x.shape, F32),
        compiler_params=_params("parallel", "parallel"),
        name="headnorm_rope",
    )(x, gain_t, cos_t, sin_t, e, e.T)


def _attn_kernel(sink_ref, qt_ref, k_ref, vt_ref, o_ref, *, lc):
    n_kv, n_grp, hd, qb = qt_ref.shape
    t = k_ref.shape[1]
    span = qb + 2 * WINDOW
    cols = n_grp * qb
    start = pl.program_id(1) * qb
    is_latent = start >= lc
    ws = pl.multiple_of(jnp.clip(start - WINDOW, lc, t - span), LANE)
    s_pos = ws + lax.broadcasted_iota(jnp.int32, (span, cols), 0)
    t_pos = start + lax.broadcasted_iota(jnp.int32, (span, cols), 1) % qb
    allowed = jnp.abs(t_pos - s_pos) <= jnp.where(is_latent, WINDOW, -1)
    qt = jnp.stack([jnp.concatenate([qt_ref[g, i] for i in range(n_grp)], axis=1) for g in range(n_kv)], axis=0)
    sink = jnp.stack([jnp.concatenate([jnp.full((1, qb), sink_ref[g * n_grp + i], F32) for i in range(n_grp)], axis=1)
                      for g in range(n_kv)], axis=0)
    s_ctx = jnp.einsum("gkd,gdq->gkq", k_ref[:, 0:lc, :], qt, preferred_element_type=F32)
    s_win = jnp.einsum("gkd,gdq->gkq", k_ref[:, pl.ds(ws, span), :], qt, preferred_element_type=F32)
    s_win = jnp.where(allowed, s_win, NEG_BIG)
    m = jnp.maximum(jnp.maximum(jnp.max(s_ctx, axis=1, keepdims=True), jnp.max(s_win, axis=1, keepdims=True)), sink)
    p_ctx = jnp.exp(s_ctx - m)
    p_win = jnp.exp(s_win - m)
    den = jnp.sum(p_ctx, axis=1, keepdims=True) + jnp.sum(p_win, axis=1, keepdims=True) + jnp.exp(sink - m)
    o = (jnp.einsum("gdk,gkq->gdq", vt_ref[:, :, 0:lc], p_ctx.astype(BF16), preferred_element_type=F32)
         + jnp.einsum("gdk,gkq->gdq", vt_ref[:, :, pl.ds(ws, span)], p_win.astype(BF16), preferred_element_type=F32))
    o = (o / den).astype(o_ref.dtype)
    for g in range(n_kv):
        for i in range(n_grp):
            o_ref[g, i] = o[g, :, i * qb:(i + 1) * qb]


def _attention(q, k, v, sink, lc):
    n_batch, t, qw = q.shape
    hd = SWA_HEAD_DIM
    n_kv = k.shape[-1] // hd
    n_grp = qw // (n_kv * hd)
    qt = q.astype(BF16).reshape(n_batch, t, n_kv, n_grp, hd).transpose(0, 2, 3, 4, 1)
    kg = k.astype(BF16).reshape(n_batch, t, n_kv, hd).transpose(0, 2, 1, 3)
    vt = v.astype(BF16).reshape(n_batch, t, n_kv, hd).transpose(0, 2, 3, 1)
    out = pl.pallas_call(
        functools.partial(_attn_kernel, lc=lc),
        grid=(n_batch, t // ATTN_BLOCK),
        in_specs=[
            pl.BlockSpec(memory_space=pltpu.SMEM),
            pl.BlockSpec((None, n_kv, n_grp, hd, ATTN_BLOCK), lambda b, j: (b, 0, 0, 0, j)),
            pl.BlockSpec((None, n_kv, t, hd), lambda b, j: (b, 0, 0, 0)),
            pl.BlockSpec((None, n_kv, hd, t), lambda b, j: (b, 0, 0, 0)),
        ],
        out_specs=pl.BlockSpec((None, n_kv, n_grp, hd, ATTN_BLOCK), lambda b, j: (b, 0, 0, 0, j)),
        out_shape=jax.ShapeDtypeStruct((n_batch, n_kv, n_grp, hd, t), BF16),
        compiler_params=_params("parallel", "parallel"),
        name="window_attention",
    )(sink, qt, kg, vt)
    return out.transpose(0, 4, 1, 2, 3).reshape(n_batch, t, qw)


def _swa_tables(lc, s):
    rows = jnp.repeat(jnp.arange(s // GRID_W), GRID_W).astype(F32)
    cols = jnp.tile(jnp.arange(GRID_W), s // GRID_W).astype(F32)
    n_freq = SWA_HEAD_DIM // 4
    inv = ROPE_BASE ** (-jnp.arange(n_freq, dtype=F32) / n_freq)
    ang = jnp.concatenate([rows[:, None] * inv[None, :], cols[:, None] * inv[None, :]], axis=-1)
    cos, sin = jnp.cos(ang), jnp.sin(ang)
    reps = LANE // SWA_HEAD_DIM
    cos_t = jnp.tile(jnp.concatenate([cos, cos], axis=-1), (1, reps))
    sin_t = jnp.tile(jnp.concatenate([-sin, sin], axis=-1), (1, reps))
    cos_t = jnp.concatenate([jnp.ones((lc, LANE), F32), cos_t], axis=0)
    sin_t = jnp.concatenate([jnp.zeros((lc, LANE), F32), sin_t], axis=0)
    return cos_t, sin_t


def _window_attention_mixer(h, mod, gain, w_qkv, q_gain, k_gain, sink, w_out, lc):
    n_batch, t, d = h.shape
    n_heads = sink.shape[0]
    q_w = n_heads * SWA_HEAD_DIM
    kv_w = (w_qkv.shape[1] - q_w) // 2
    cos_t, sin_t = _swa_tables(lc, t - lc)
    w_bf = w_qkv.astype(BF16)
    q = _proj(h, mod, gain, w_bf[:, :q_w], lc)
    k = _proj(h, mod, gain, w_bf[:, q_w:q_w + kv_w], lc)
    v = _proj(h, mod, gain, w_bf[:, q_w + kv_w:], lc)
    q = _headnorm_rope(q, q_gain, cos_t, sin_t, SWA_HEAD_DIM ** -0.5, lc)
    k = _headnorm_rope(k, k_gain, cos_t, sin_t, 1.0, lc)
    y = _attention(q, k, v, sink, lc)
    return _mixer_out(h, y, mod, w_out.astype(BF16), lc)


def _rope_halves(x, cos, sin):
    half = x.shape[1] // 2
    x1, x2 = x[:, :half], x[:, half:]
    return jnp.concatenate([x1 * cos - x2 * sin, x1 * sin + x2 * cos], axis=1)


def _ret_kernel(qf_ref, kf_ref, vf_ref, cosf_ref, sinf_ref, qb_ref, kb_ref, vb_ref, cosb_ref, sinb_ref,
                of_ref, ob_ref, sf_ref, sb_ref):
    c = RET_CHUNK
    rows = qf_ref.shape[0]
    dk = qf_ref.shape[1]
    head = pl.program_id(1)

    @pl.when(pl.program_id(2) == 0)
    def _():
        sf_ref[...] = jnp.zeros_like(sf_ref)
        sb_ref[...] = jnp.zeros_like(sb_ref)

    hv = jnp.zeros((1, 1), F32) + head.astype(F32)
    lg = jnp.log(1.0 - jnp.exp((-5.0 - hv) * math.log(2.0)))
    ri = lax.broadcasted_iota(jnp.int32, (c, c), 0)
    ci = lax.broadcasted_iota(jnp.int32, (c, c), 1)
    pos = lax.broadcasted_iota(jnp.int32, (c, 1), 0).astype(F32)
    chunk_decay = jnp.exp(lg * float(c))

    def run(q_ref, k_ref, v_ref, cos_ref, sin_ref, o_ref, s_ref, reverse):
        rel = ((ci - ri) if reverse else (ri - ci))
        decay = jnp.exp(jnp.where(rel >= 0, lg * rel.astype(F32), NEG_BIG))
        p = (c - 1.0 - pos) if reverse else pos
        q_fac = jnp.exp(lg * (p + 1.0))
        k_fac = jnp.exp(lg * (c - 1.0 - p))
        n_sub = rows // c
        order = range(n_sub - 1, -1, -1) if reverse else range(n_sub)
        for i in order:
            sl = slice(i * c, (i + 1) * c)
            q = _rope_halves(q_ref[sl, :].astype(F32), cos_ref[sl, :], sin_ref[sl, :])
            k = _rope_halves(k_ref[sl, :].astype(F32), cos_ref[sl, :], sin_ref[sl, :]) * (dk ** -0.5)
            v = v_ref[sl, :].astype(BF16)
            scores = _dot_nt(q.astype(BF16), k.astype(BF16)) * decay
            state = s_ref[...]
            o = _dot(scores.astype(BF16), v) + _dot((q * q_fac).astype(BF16), state.astype(BF16))
            o_ref[sl, :] = o.astype(o_ref.dtype)
            s_ref[...] = state * chunk_decay + _dot_tn((k * k_fac).astype(BF16), v)

    run(qf_ref, kf_ref, vf_ref, cosf_ref, sinf_ref, of_ref, sf_ref, False)
    run(qb_ref, kb_ref, vb_ref, cosb_ref, sinb_ref, ob_ref, sb_ref, True)


def _retention_core(proj, cos_t, sin_t, lc):
    n_batch, t, width = proj.shape
    nh = RET_HEADS
    dk = width // (8 * nh)
    dv = 2 * dk
    rows = lc
    n_steps = t // rows
    n_ctx = lc // rows

    def fwd(b, h, s):
        return s

    def bwd(b, h, s):
        return jnp.where(s < n_ctx, n_ctx - 1 - s, n_steps - 1 - s + n_ctx)

    def specs(step):
        return [
            pl.BlockSpec((None, rows, dk), lambda b, h, s: (b, step(b, h, s), h)),
            pl.BlockSpec((None, rows, dk), lambda b, h, s: (b, step(b, h, s), nh + h)),
            pl.BlockSpec((None, rows, dv), lambda b, h, s: (b, step(b, h, s), nh + h)),
            pl.BlockSpec((rows, dk // 2), lambda b, h, s: (step(b, h, s), 0)),
            pl.BlockSpec((rows, dk // 2), lambda b, h, s: (step(b, h, s), 0)),
        ]

    out_sds = jax.ShapeDtypeStruct((n_batch, t, nh * dv), BF16)
    return pl.pallas_call(
        _ret_kernel,
        grid=(n_batch, nh, n_steps),
        in_specs=specs(fwd) + specs(bwd),
        out_specs=[
            pl.BlockSpec((None, rows, dv), lambda b, h, s: (b, fwd(b, h, s), h)),
            pl.BlockSpec((None, rows, dv), lambda b, h, s: (b, bwd(b, h, s), h)),
        ],
        out_shape=[out_sds, out_sds],
        scratch_shapes=[pltpu.VMEM((dk, dv), F32), pltpu.VMEM((dk, dv), F32)],
        compiler_params=_params("parallel", "parallel", "arbitrary"),
        name="retention",
    )(proj, proj, proj, cos_t, sin_t, proj, proj, proj, cos_t, sin_t)


def _ret_out_kernel(x_ref, of_ref, ob_ref, gf_ref, gb_ref, gain_ref, mod_ref, w_ref, o_ref, *, n_heads):
    dv = of_ref.shape[1] // n_heads
    parts = []
    for hh in range(n_heads):
        sl = slice(hh * dv, (hh + 1) * dv)

        def group_norm(o, gain):
            mu = jnp.mean(o, axis=-1, keepdims=True)
            xc = o - mu
            var = jnp.mean(xc * xc, axis=-1, keepdims=True)
            return xc * lax.rsqrt(var + EPS) * gain

        y = (_silu(gf_ref[:, sl].astype(F32)) * group_norm(of_ref[:, sl].astype(F32), gain_ref[0:1, sl])
             + _silu(gb_ref[:, sl].astype(F32)) * group_norm(ob_ref[:, sl].astype(F32), gain_ref[1:2, sl]))
        parts.append(y.astype(BF16))
    y = _dot(jnp.concatenate(parts, axis=1), w_ref[...])
    o_ref[...] = x_ref[...] + mod_ref[2:3, :] * y


def _retention_out(h, o_f, o_b, proj, gn_gain, mod, w, lc):
    n_batch, t, d = h.shape
    vw = o_f.shape[-1]
    tm = _row_tile(lc, t - lc)
    gate_f = proj.shape[-1] // vw - 2
    return pl.pallas_call(
        functools.partial(_ret_out_kernel, n_heads=RET_HEADS),
        grid=(n_batch, t // tm),
        in_specs=[
            pl.BlockSpec((None, tm, d), lambda b, i: (b, i, 0)),
            pl.BlockSpec((None, tm, vw), lambda b, i: (b, i, 0)),
            pl.BlockSpec((None, tm, vw), lambda b, i: (b, i, 0)),
            pl.BlockSpec((None, tm, vw), lambda b, i: (b, i, gate_f)),
            pl.BlockSpec((None, tm, vw), lambda b, i: (b, i, gate_f + 1)),
            pl.BlockSpec((2, vw), lambda b, i: (0, 0)),
            pl.BlockSpec((None, 3, d), _mod_index(n_batch, lc // tm)),
            _resident((vw, d), lambda b, i: (0, 0)),
        ],
        out_specs=pl.BlockSpec((None, tm, d), lambda b, i: (b, i, 0)),
        out_shape=jax.ShapeDtypeStruct(h.shape, F32),
        compiler_params=_params("parallel", "parallel"),
        name="retention_out",
    )(h, o_f, o_b, proj, proj, gn_gain, mod, w)


def _ret_tables(lc, s, n_freq):
    inv = ROPE_BASE ** (-jnp.arange(n_freq, dtype=F32) / n_freq)
    ang = jnp.arange(s, dtype=F32)[:, None] * inv[None, :]
    cos_t = jnp.concatenate([jnp.ones((lc, n_freq), F32), jnp.cos(ang)], axis=0)
    sin_t = jnp.concatenate([jnp.zeros((lc, n_freq), F32), jnp.sin(ang)], axis=0)
    return cos_t, sin_t


def _retention_mixer(h, mod, gain, w_in, gn_gain, w_out, lc):
    n_batch, t, d = h.shape
    dk = w_in.shape[1] // (8 * RET_HEADS)
    cos_t, sin_t = _ret_tables(lc, t - lc, dk // 2)
    proj = _proj(h, mod, gain, w_in.astype(BF16), lc, out_dtype=BF16)
    o_f, o_b = _retention_core(proj, cos_t, sin_t, lc)
    return _retention_out(h, o_f, o_b, proj, gn_gain, mod, w_out.astype(BF16), lc)


def kernel(x, c, ctx, c_ctx, ada_w, ada_b, norm_g, ffn_w1, ffn_w2, dn_w_in, dn_conv, dn_a_log, dn_dt_bias,
           dn_o_gain, dn_w_out, swa_w_qkv, swa_q_gain, swa_k_gain, swa_sink, swa_w_out, ret_w_in, ret_gn_gain,
           ret_w_out):
    n_batch, s, d = x.shape
    lc = ctx.shape[1]
    depth = ada_w.shape[0]
    n_mod = ada_w.shape[2] // d

    bp = -(-(n_batch + 1) // SUBLANE) * SUBLANE
    cond = jnp.concatenate([c, c_ctx[None, :], jnp.zeros((bp - n_batch - 1, d), F32)], axis=0)
    mods = _ada_mods(cond, ada_w, ada_b).reshape(depth, bp, n_mod, d)

    w1 = ffn_w1.astype(BF16)
    w2 = ffn_w2.astype(BF16)
    h = None
    for i in range(depth):
        kind, slot = i % N_MIXERS, i // N_MIXERS
        sub = [mods[i, :, 3 * j:3 * j + 3, :] for j in range(3)]
        if i == 0:
            h = _ffn_joining(ctx, x, sub[0], norm_g[i, 0], w1[i, 0], w2[i, 0])
        else:
            h = _ffn(h, sub[0], norm_g[i, 0], w1[i, 0], w2[i, 0], lc)
        if kind == 0:
            h = _deltanet_mixer(h, sub[1], norm_g[i, 1], dn_w_in[slot], dn_conv[slot], dn_a_log[slot],
                                dn_dt_bias[slot], dn_o_gain[slot], dn_w_out[slot], lc)
        elif kind == 1:
            h = _window_attention_mixer(h, sub[1], norm_g[i, 1], swa_w_qkv[slot], swa_q_gain[slot],
                                        swa_k_gain[slot], swa_sink[slot], swa_w_out[slot], lc)
        else:
            h = _retention_mixer(h, sub[1], norm_g[i, 1], ret_w_in[slot], ret_gn_gain[slot], ret_w_out[slot], lc)
        h = _ffn(h, sub[2], norm_g[i, 2], w1[i, 1], w2[i, 1], lc, latent_only=(i == depth - 1))
    return h
```

```python
import functools
import math

import jax
import jax.numpy as jnp
from jax import lax
from jax.experimental import pallas as pl
from jax.experimental.pallas import tpu as pltpu

F32 = jnp.float32
BF16 = jnp.bfloat16

N_MIXERS = 3
GRID_W = 64
ROPE_BASE = 10000.0
EPS = 1e-6
FFN_RESIDUAL = 0.5
DN_HEAD_DIM = 128
DN_CHUNK = 64
DN_INV_BLOCK = 16
DN_CHUNK_BATCHES = (18, 12, 6, 5, 4, 3, 2, 1)
SWA_HEAD_DIM = 64
SWA_GROUP = 4
WINDOW = 128
ATTN_BLOCK = 128
RET_HEADS = 4
RET_CHUNK = 128

LANE = 128
SUBLANE = 8
VMEM_LIMIT_BYTES = 52 * 1024 * 1024
NEG_BIG = -1e30


def _params(*semantics):
    return pltpu.CompilerParams(dimension_semantics=semantics, vmem_limit_bytes=VMEM_LIMIT_BYTES)


def _resident(block_shape, index_map):
    return pl.BlockSpec(block_shape, index_map, pipeline_mode=pl.Buffered(1))


def _dot(a, b):
    return jnp.dot(a, b, preferred_element_type=F32)


def _dot_nt(a, b):
    return lax.dot_general(a, b, (((1,), (1,)), ((), ())), preferred_element_type=F32)


def _dot_tn(a, b):
    return lax.dot_general(a, b, (((0,), (0,)), ((), ())), preferred_element_type=F32)


def _split2(x):
    hi = x.astype(BF16)
    lo = (x - hi.astype(F32)).astype(BF16)
    return hi, lo


def _split3(x):
    hi = x.astype(BF16)
    r = x - hi.astype(F32)
    mid = r.astype(BF16)
    lo = (r - mid.astype(F32)).astype(BF16)
    return hi, mid, lo


def _dot_x3(a, b):
    ah, al = _split2(a)
    bh, bl = _split2(b)
    return _dot(ah, bh) + (_dot(ah, bl) + _dot(al, bh))


def _sigmoid(x):
    return 1.0 / (1.0 + jnp.exp(-x))


def _silu(x):
    return x * _sigmoid(x)


def _softplus(x):
    return jnp.maximum(x, 0.0) + jnp.log(1.0 + jnp.exp(-jnp.abs(x)))


def _mod_norm(x, gain, shift, scale):
    ms = jnp.mean(x * x, axis=-1, keepdims=True)
    return (x * lax.rsqrt(ms + EPS)) * (gain * (1.0 + scale)) + shift


def _ada_kernel(c_ref, w_ref, b_ref, o_ref):
    o_ref[...] = _dot_x3(_silu(c_ref[...]), w_ref[...]) + b_ref[...]


def _ada_mods(cond, ada_w, ada_b):
    depth, d, nd = ada_w.shape
    bp = cond.shape[0]
    tn = d
    return pl.pallas_call(
        _ada_kernel,
        grid=(depth, nd // tn),
        in_specs=[
            pl.BlockSpec((bp, d), lambda l, n: (0, 0)),
            pl.BlockSpec((None, d, tn), lambda l, n: (l, 0, n)),
            pl.BlockSpec((None, 1, tn), lambda l, n: (l, 0, n)),
        ],
        out_specs=pl.BlockSpec((None, bp, tn), lambda l, n: (l, 0, n)),
        out_shape=jax.ShapeDtypeStruct((depth, bp, nd), F32),
        compiler_params=_params("parallel", "parallel"),
        name="ada_mods",
    )(cond, ada_w, ada_b.reshape(depth, 1, nd))


def _row_tile(lc, s):
    tm = 256
    while lc % tm or s % tm:
        tm //= 2
    return tm


def _mod_index(n_batch, n_ctx_tiles):
    return lambda b, t: (jnp.where(t < n_ctx_tiles, n_batch, b), 0, 0)


def _ffn_rows(x, mod_ref, g_ref, w1_ref, w2_ref, o_ref):
    mod = mod_ref[...]
    hn = _mod_norm(x, g_ref[...], mod[0:1], mod[1:2]).astype(BF16)
    f = w2_ref.shape[0]
    a = _dot(hn, w1_ref[:, :f])
    b = _dot(hn, w1_ref[:, f:])
    y = _dot((_silu(a) * b).astype(BF16), w2_ref[...])
    o_ref[...] = x + (FFN_RESIDUAL * mod[2:3]) * y


def _ffn_kernel(x_ref, mod_ref, g_ref, w1_ref, w2_ref, o_ref):
    _ffn_rows(x_ref[...], mod_ref, g_ref, w1_ref, w2_ref, o_ref)


def _ffn_joining_kernel(ctx_ref, x_ref, mod_ref, g_ref, w1_ref, w2_ref, o_ref, *, n_ctx_tiles):
    tile = pl.program_id(1) + jnp.zeros(x_ref.shape, jnp.int32)
    _ffn_rows(jnp.where(tile < n_ctx_tiles, ctx_ref[...], x_ref[...]), mod_ref, g_ref, w1_ref, w2_ref, o_ref)


def _ffn_after_proj_kernel(x_ref, y_ref, wo_ref, mix_mod_ref, mod_ref, g_ref, w1_ref, w2_ref, o_ref):
    x = x_ref[...] + mix_mod_ref[2:3, :] * _dot(y_ref[...].astype(BF16), wo_ref[...])
    _ffn_rows(x, mod_ref, g_ref, w1_ref, w2_ref, o_ref)


def _ffn_after_retention_kernel(x_ref, of_ref, ob_ref, gf_ref, gb_ref, gain_ref, wo_ref, mix_mod_ref,
                                mod_ref, g_ref, w1_ref, w2_ref, o_ref, *, n_heads):
    y = _retention_finish(of_ref, ob_ref, gf_ref, gb_ref, gain_ref, n_heads)
    x = x_ref[...] + mix_mod_ref[2:3, :] * _dot(y, wo_ref[...])
    _ffn_rows(x, mod_ref, g_ref, w1_ref, w2_ref, o_ref)


def _ffn(h, mod, gain, w1, w2, lc, latent_only=False, pending=None):
    n_batch, t, d = h.shape
    f = w2.shape[0]
    tm = _row_tile(lc, t - lc)
    skip = lc // tm if latent_only else 0
    n_tiles = t // tm - skip

    def rows(width, col=0):
        return pl.BlockSpec((None, tm, width), lambda b, i: (b, i + skip, col))

    def whole(shape):
        return _resident(shape, lambda b, i: (0, 0))

    mod_spec = pl.BlockSpec((None, 3, d), _mod_index(n_batch, lc // tm - skip))
    if pending is None:
        body, specs, args = _ffn_kernel, [rows(d)], [h]
    elif pending[0] == "proj":
        _, y, w_out, mix_mod = pending
        k = y.shape[-1]
        body = _ffn_after_proj_kernel
        specs = [rows(d), rows(k), whole((k, d)), mod_spec]
        args = [h, y, w_out, mix_mod]
    else:
        _, o_f, o_b, proj, gn_gain, w_out, mix_mod = pending
        vw = o_f.shape[-1]
        gate_f = proj.shape[-1] // vw - 2
        body = functools.partial(_ffn_after_retention_kernel, n_heads=RET_HEADS)
        specs = [rows(d), rows(vw), rows(vw), rows(vw, gate_f), rows(vw, gate_f + 1),
                 pl.BlockSpec((2, vw), lambda b, i: (0, 0)), whole((vw, d)), mod_spec]
        args = [h, o_f, o_b, proj, proj, gn_gain, w_out, mix_mod]
    return pl.pallas_call(
        body,
        grid=(n_batch, n_tiles),
        in_specs=specs + [mod_spec, pl.BlockSpec((1, d), lambda b, i: (0, 0)), whole((d, 2 * f)), whole((f, d))],
        out_specs=pl.BlockSpec((None, tm, d), lambda b, i: (b, i, 0)),
        out_shape=jax.ShapeDtypeStruct((n_batch, n_tiles * tm, d), F32),
        compiler_params=_params("parallel", "parallel"),
        name="ffn",
    )(*args, mod, gain.reshape(1, d), w1, w2)


def _ffn_joining(ctx, x, mod, gain, w1, w2):
    n_batch, s, d = x.shape
    lc = ctx.shape[1]
    f = w2.shape[0]
    tm = _row_tile(lc, s)
    nct = lc // tm
    return pl.pallas_call(
        functools.partial(_ffn_joining_kernel, n_ctx_tiles=nct),
        grid=(n_batch, (lc + s) // tm),
        in_specs=[
            pl.BlockSpec((None, tm, d), lambda b, i: (b, jnp.minimum(i, nct - 1), 0)),
            pl.BlockSpec((None, tm, d), lambda b, i: (b, jnp.maximum(i - nct, 0), 0)),
            pl.BlockSpec((None, 3, d), _mod_index(n_batch, nct)),
            pl.BlockSpec((1, d), lambda b, i: (0, 0)),
            _resident((d, 2 * f), lambda b, i: (0, 0)),
            _resident((f, d), lambda b, i: (0, 0)),
        ],
        out_specs=pl.BlockSpec((None, tm, d), lambda b, i: (b, i, 0)),
        out_shape=jax.ShapeDtypeStruct((n_batch, lc + s, d), F32),
        compiler_params=_params("parallel", "parallel"),
        name="ffn_joining",
    )(ctx, x, mod, gain.reshape(1, d), w1, w2)


def _proj_kernel(x_ref, mod_ref, g_ref, w_ref, o_ref):
    mod = mod_ref[...]
    hn = _mod_norm(x_ref[...], g_ref[...], mod[0:1], mod[1:2]).astype(BF16)
    o_ref[...] = _dot(hn, w_ref[...]).astype(o_ref.dtype)


PROJ_WEIGHT_TILE_BYTES = 16 * 1024 * 1024


def _col_tile(w):
    d, n = w.shape
    if w.size * w.dtype.itemsize <= PROJ_WEIGHT_TILE_BYTES:
        return n
    tn = 2048
    while n % tn:
        tn -= LANE
    return tn


def _proj(h, mod, gain, w, lc, out_dtype=F32):
    n_batch, t, d = h.shape
    n = w.shape[1]
    tm = _row_tile(lc, t - lc)
    tn = _col_tile(w)
    mod_idx = _mod_index(n_batch, lc // tm)
    w_spec = (_resident if tn == n else pl.BlockSpec)((d, tn), lambda j, b, i: (0, j))
    return pl.pallas_call(
        _proj_kernel,
        grid=(n // tn, n_batch, t // tm),
        in_specs=[
            pl.BlockSpec((None, tm, d), lambda j, b, i: (b, i, 0)),
            pl.BlockSpec((None, 3, d), lambda j, b, i: mod_idx(b, i)),
            pl.BlockSpec((1, d), lambda j, b, i: (0, 0)),
            w_spec,
        ],
        out_specs=pl.BlockSpec((None, tm, tn), lambda j, b, i: (b, i, j)),
        out_shape=jax.ShapeDtypeStruct((n_batch, t, n), out_dtype),
        compiler_params=_params("parallel", "parallel", "parallel"),
        name="mixer_in",
    )(h, mod, gain.reshape(1, d), w)


def _bmm(a, b):
    return jnp.einsum("gik,gkj->gij", a.astype(BF16), b.astype(BF16), preferred_element_type=F32)


def _pair_mm(x, y, left):
    y = y.astype(BF16)
    zero = jnp.zeros_like(y)
    y_diag = jnp.concatenate([jnp.where(left, y, zero), jnp.where(left, zero, y)], axis=1)
    return jnp.einsum("gik,gkj->gij", x.astype(BF16), y_diag, preferred_element_type=F32)


def _unit_tri_inverse(a, blk_mask, eye_f, left):
    c = a.shape[1]
    d = jnp.where(blk_mask, a, 0.0)
    e = a - d
    p = eye_f - d
    dk = d
    for _ in range(DN_INV_BLOCK.bit_length() - 2):
        dk = _pair_mm(dk, dk, left)
        p = _pair_mm(p, eye_f + dk, left)
    n = _pair_mm(p, e, left)
    x = eye_f - n
    nk = n
    for _ in range((c // DN_INV_BLOCK).bit_length() - 2):
        nk = _pair_mm(nk, nk, left)
        x = _pair_mm(x, eye_f + nk, left)
    return _pair_mm(x, p, left)


def _dn_kernel(q_ref, k_ref, v_ref, z_ref, cwq_ref, cwk_ref, cwv_ref, gcol_ref, grow_ref, og_ref,
               y_ref,
               pad_s, q_s, k_s, v_s, kq_s, b_s, a_s, o_s, *, lc):
    t = q_ref.shape[0]
    c = DN_CHUNK
    nc = t // c
    ncc = lc // c
    rb = _row_tile(lc, t - lc)

    zeros8 = jnp.zeros((SUBLANE, DN_HEAD_DIM), F32)

    def conv_act(x_ref, cw_ref, dst, normalise):
        pad_s[0:SUBLANE, :] = zeros8
        pad_s[SUBLANE:SUBLANE + lc, :] = x_ref[0:lc, :]
        pad_s[SUBLANE + lc:2 * SUBLANE + lc, :] = zeros8
        pad_s[2 * SUBLANE + lc:2 * SUBLANE + t, :] = x_ref[lc:t, :]
        pad_s[2 * SUBLANE + t:3 * SUBLANE + t, :] = zeros8
        cw = cw_ref[...]
        n_tap = cw.shape[0]
        for r0 in range(0, t, rb):
            base = r0 + (SUBLANE if r0 < lc else 2 * SUBLANE)
            acc = None
            for j in range(n_tap):
                lo = base + j - n_tap // 2
                term = pad_s[lo:lo + rb, :] * cw[j:j + 1, :]
                acc = term if acc is None else acc + term
            act = _silu(acc)
            if normalise:
                act = act * lax.rsqrt(jnp.sum(act * act, axis=-1, keepdims=True) + EPS)
            dst[r0:r0 + rb, :] = act

    conv_act(q_ref, cwq_ref, q_s, True)
    conv_act(k_ref, cwk_ref, k_s, True)
    conv_act(v_ref, cwv_ref, v_s, False)

    ri = lax.broadcasted_iota(jnp.int32, (c, 2 * c), 0)
    li = lax.broadcasted_iota(jnp.int32, (c, 2 * c), 1)
    left = li < c
    ci = jnp.where(left, li, li - c)
    eye_f = jnp.where(ri == ci, 1.0, 0.0).astype(F32)
    blk_mask = (ri // DN_INV_BLOCK) == (ci // DN_INV_BLOCK)
    lag = jnp.where(left, ri - ci, ci - ri)
    incl = lag >= 0
    strict = lag > 0
    q_scale = DN_HEAD_DIM ** -0.5
    g = max(n for n in DN_CHUNK_BATCHES if nc % n == 0)

    def phase1(it, carry):
        ch0 = it * g
        rows = pl.ds(pl.multiple_of(ch0 * c, c), g * c)
        q = q_s[rows, :].reshape(g, c, DN_HEAD_DIM) * q_scale
        k = k_s[rows, :].reshape(g, c, DN_HEAD_DIM)
        v = v_s[rows, :].reshape(g, c, DN_HEAD_DIM)
        gcol = gcol_ref[rows, :].reshape(g, c, 4)
        grow = grow_ref[pl.ds(ch0, g)]
        k_bf = k.astype(BF16)
        gcs = (gcol[:, :, 0:1], gcol[:, :, 2:3])
        kbs = (k * gcol[:, :, 1:2], k * gcol[:, :, 3:4])
        decay = jnp.exp(jnp.where(incl, jnp.where(left, gcs[0], gcs[1]) - grow, NEG_BIG))
        kq = jnp.einsum("gic,gjc->gij", jnp.concatenate([kbs[0], kbs[1], q], axis=1).astype(BF16),
                        jnp.concatenate([k_bf, k_bf], axis=1), preferred_element_type=F32)
        a = jnp.where(strict, jnp.where(left, kq[:, :c], kq[:, c:2 * c]) * decay, 0.0)
        qk = (kq[:, 2 * c:] * decay).astype(BF16)
        tinv = _unit_tri_inverse(a, blk_mask, eye_f, left).astype(BF16)
        for d in range(2):
            gc, kb, beta = gcs[d], kbs[d], gcol[:, :, 2 * d + 1:2 * d + 2]

            def own_rows(x):
                zero = jnp.zeros_like(x)
                return jnp.concatenate([x, zero] if d == 0 else [zero, x], axis=1)

            eg = jnp.exp(gc)
            wu = jnp.einsum("gik,gkj->gij", tinv, own_rows(jnp.concatenate([kb * eg, v * beta], axis=2).astype(BF16)),
                            preferred_element_type=F32).astype(BF16)
            g_tot = gc[:, c - 1:c, :] if d == 0 else gc[:, 0:1, :]
            kd = (k * jnp.exp(g_tot - gc)).astype(BF16)
            kwu = jnp.einsum("gck,gcn->gkn", kd, wu, preferred_element_type=F32)
            qwu = jnp.einsum("gik,gkj->gij", qk, own_rows(wu), preferred_element_type=F32)
            kq_s[d, pl.ds(ch0, g), 0:DN_HEAD_DIM, :] = kwu[:, :, :DN_HEAD_DIM].astype(BF16)
            kq_s[d, pl.ds(ch0, g), DN_HEAD_DIM:, :] = (q * eg - qwu[:, :, :DN_HEAD_DIM]).astype(BF16)
            b_s[d, pl.ds(ch0, g)] = kwu[:, :, DN_HEAD_DIM:]
            o_s[d, rows, :] = qwu[:, :, DN_HEAD_DIM:].reshape(g * c, DN_HEAD_DIM)
            a_s[d, pl.ds(ch0, g)] = jnp.broadcast_to(jnp.exp(g_tot), (g, SUBLANE, DN_HEAD_DIM))
        return carry

    lax.fori_loop(0, nc // g, phase1, 0)

    def phase2(s, carry):
        chunk_of = (s, jnp.where(s < ncc, ncc - 1 - s, nc - 1 - s + ncc))
        new = []
        for d in range(2):
            state = carry[d]
            ch = chunk_of[d]
            rows = pl.ds(pl.multiple_of(ch * c, c), c)
            r = _dot(kq_s[d, ch], state.astype(BF16))
            o_s[d, rows, :] = o_s[d, rows, :] + r[DN_HEAD_DIM:]
            new.append(state * a_s[d, ch][0:1, :] + (b_s[d, ch] - r[:DN_HEAD_DIM]))
        return tuple(new)

    zero_state = jnp.zeros((DN_HEAD_DIM, DN_HEAD_DIM), F32)
    lax.fori_loop(0, nc, phase2, (zero_state, zero_state))

    og = og_ref[...]
    for r0 in range(0, t, rb):
        o = o_s[0, r0:r0 + rb, :] + o_s[1, r0:r0 + rb, :]
        o = o * lax.rsqrt(jnp.mean(o * o, axis=-1, keepdims=True) + EPS) * og
        y_ref[r0:r0 + rb, :] = o * _silu(z_ref[r0:r0 + rb, :])


def _dn_gates_kernel(ab_ref, alog_ref, dtb_ref, o_ref, *, n_heads):
    t, width = ab_ref.shape
    c = DN_CHUNK
    col = lax.broadcasted_iota(jnp.int32, (1, width), 1)
    is_decay = (col % (2 * n_heads)) < n_heads
    backward = col >= 2 * n_heads
    ri = lax.broadcasted_iota(jnp.int32, (c, c), 0)
    ci = lax.broadcasted_iota(jnp.int32, (c, c), 1)
    tri_f = jnp.where(ci <= ri, 1.0, 0.0).astype(BF16)
    tri_b = jnp.where(ci >= ri, 1.0, 0.0).astype(BF16)
    neg_a = -jnp.exp(alog_ref[...])
    dtb = dtb_ref[...]
    for r0 in range(0, t, c):
        raw = ab_ref[r0:r0 + c, :]
        g = neg_a * _softplus(raw + dtb)
        g1, g2, g3 = _split3(g)
        cum_f = _dot(tri_f, g1) + (_dot(tri_f, g2) + _dot(tri_f, g3))
        cum_b = _dot(tri_b, g1) + (_dot(tri_b, g2) + _dot(tri_b, g3))
        o_ref[r0:r0 + c, :] = jnp.where(is_decay, jnp.where(backward, cum_b, cum_f), _sigmoid(raw))


def _dn_gates(ab, a_log, dt_bias):
    n_batch, t, width = ab.shape
    nh = a_log.shape[-1]

    def param_row(p):
        z = jnp.zeros((nh,), F32)
        row = jnp.concatenate([p[0], z, p[1], z])
        return jnp.pad(row, (0, width - 4 * nh)).reshape(1, width)

    return pl.pallas_call(
        functools.partial(_dn_gates_kernel, n_heads=nh),
        grid=(n_batch,),
        in_specs=[
            pl.BlockSpec((None, t, width), lambda b: (b, 0, 0)),
            pl.BlockSpec((1, width), lambda b: (0, 0)),
            pl.BlockSpec((1, width), lambda b: (0, 0)),
        ],
        out_specs=pl.BlockSpec((None, t, width), lambda b: (b, 0, 0)),
        out_shape=jax.ShapeDtypeStruct(ab.shape, F32),
        compiler_params=_params("parallel"),
        name="deltanet_gates",
    )(ab, param_row(a_log), param_row(dt_bias))


def _dn_core(qkvz, gate_cols, gate_rows, conv_w, o_gain, lc):
    n_batch, t, width = qkvz.shape
    hd = DN_HEAD_DIM
    nh = width // (4 * hd)
    nc = t // DN_CHUNK

    def col_block(offset):
        return pl.BlockSpec((None, t, hd), lambda b, h: (b, 0, offset + h))

    def conv_block(offset):
        return pl.BlockSpec((conv_w.shape[0], hd), lambda b, h: (0, offset + h))

    return pl.pallas_call(
        functools.partial(_dn_kernel, lc=lc),
        grid=(n_batch, nh),
        in_specs=[
            col_block(0), col_block(nh), col_block(2 * nh), col_block(3 * nh),
            conv_block(0), conv_block(nh), conv_block(2 * nh),
            pl.BlockSpec((None, None, t, 4), lambda b, h: (b, h, 0, 0)),
            pl.BlockSpec((None, None, nc, 1, 2 * DN_CHUNK), lambda b, h: (b, h, 0, 0, 0)),
            pl.BlockSpec((1, hd), lambda b, h: (0, 0)),
        ],
        out_specs=pl.BlockSpec((None, t, hd), lambda b, h: (b, 0, h)),
        out_shape=jax.ShapeDtypeStruct((n_batch, t, nh * hd), F32),
        scratch_shapes=[
            pltpu.VMEM((t + 3 * SUBLANE, hd), F32),
            pltpu.VMEM((t, hd), F32),
            pltpu.VMEM((t, hd), F32),
            pltpu.VMEM((t, hd), F32),
            pltpu.VMEM((2, nc, hd + DN_CHUNK, hd), BF16),
            pltpu.VMEM((2, nc, hd, hd), F32),
            pltpu.VMEM((2, nc, SUBLANE, hd), F32),
            pltpu.VMEM((2, t, hd), F32),
        ],
        compiler_params=_params("parallel", "parallel"),
        name="deltanet",
    )(qkvz, qkvz, qkvz, qkvz, conv_w, conv_w, conv_w, gate_cols, gate_rows, o_gain.reshape(1, hd))


def _deltanet_mixer(h, mod, gain, w_in, conv_w, a_log, dt_bias, o_gain, w_out, lc):
    n_batch, t, d = h.shape
    nh = a_log.shape[-1]
    width = 4 * nh * DN_HEAD_DIM
    qkvz = _proj(h, mod, gain, w_in[:, :width].astype(BF16), lc)
    w_ab = jnp.pad(w_in[:, width:], ((0, 0), (0, LANE - 4 * nh))).astype(BF16)
    gates = _dn_gates(_proj(h, mod, gain, w_ab, lc), a_log, dt_bias)[:, :, :4 * nh]
    gates = gates.reshape(n_batch, t, 2, 2, nh)
    gate_cols = gates.transpose(0, 4, 1, 2, 3).reshape(n_batch, nh, t, 4)
    gate_rows = (gates[:, :, :, 0, :].transpose(0, 3, 2, 1)
                 .reshape(n_batch, nh, 2, t // DN_CHUNK, DN_CHUNK).transpose(0, 1, 3, 2, 4)
                 .reshape(n_batch, nh, t // DN_CHUNK, 1, 2 * DN_CHUNK))
    y = _dn_core(qkvz, gate_cols, gate_rows, conv_w, o_gain, lc)
    return ("proj", y, w_out.astype(BF16), mod)


def _headnorm_rope_kernel(x_ref, gain_ref, cos_ref, sin_ref, e_ref, et_ref, o_ref, *, scale):
    x = x_ref[...]
    width = x.shape[1]
    sq_hi, sq_lo = _split2(x * x)
    ss = _dot(sq_hi, e_ref[...]) + _dot(sq_lo, e_ref[...])
    inv_hi, inv_lo = _split2(lax.rsqrt(ss * (1.0 / SWA_HEAD_DIM) + EPS))
    inv = _dot(inv_hi, et_ref[...]) + _dot(inv_lo, et_ref[...])
    lane = lax.broadcasted_iota(jnp.int32, (1, LANE), 1)
    first_half = (lane % SWA_HEAD_DIM) < SWA_HEAD_DIM // 2
    gain = gain_ref[...]
    cos = cos_ref[...]
    sin = sin_ref[...]
    for s in range(width // LANE):
        xs = x[:, s * LANE:(s + 1) * LANE] * inv[:, s * LANE:(s + 1) * LANE] * gain
        partner = jnp.where(first_half, pltpu.roll(xs, LANE - SWA_HEAD_DIM // 2, 1),
                            pltpu.roll(xs, SWA_HEAD_DIM // 2, 1))
        o_ref[:, s * LANE:(s + 1) * LANE] = (xs * cos + partner * sin) * scale


def _headnorm_rope(x, gain, cos_t, sin_t, scale, lc):
    n_batch, t, width = x.shape
    tm = _row_tile(lc, t - lc)
    heads = lax.broadcasted_iota(jnp.int32, (width, LANE), 0) // SWA_HEAD_DIM
    e = (heads == lax.broadcasted_iota(jnp.int32, (width, LANE), 1)).astype(BF16)
    gain_t = jnp.tile(gain, LANE // SWA_HEAD_DIM).reshape(1, LANE)
    return pl.pallas_call(
        functools.partial(_headnorm_rope_kernel, scale=scale),
        grid=(n_batch, t // tm),
        in_specs=[
            pl.BlockSpec((None, tm, width), lambda b, i: (b, i, 0)),
            pl.BlockSpec((1, LANE), lambda b, i: (0, 0)),
            pl.BlockSpec((tm, LANE), lambda b, i: (i, 0)),
            pl.BlockSpec((tm, LANE), lambda b, i: (i, 0)),
            pl.BlockSpec((width, LANE), lambda b, i: (0, 0)),
            pl.BlockSpec((LANE, width), lambda b, i: (0, 0)),
        ],
        out_specs=pl.BlockSpec((None, tm, width), lambda b, i: (b, i, 0)),
        out_shape=jax.ShapeDtypeStruct(x.shape, F32),
        compiler_params=_params("parallel", "parallel"),
        name="headnorm_rope",
    )(x, gain_t, cos_t, sin_t, e, e.T)


def _attn_kernel(sink_ref, qt_ref, k_ref, vt_ref, o_ref, *, lc):
    n_kv, n_grp, hd, qb = qt_ref.shape
    t = k_ref.shape[1]
    span = qb + 2 * WINDOW
    cols = n_grp * qb
    start = pl.program_id(1) * qb
    is_latent = start >= lc
    ws = pl.multiple_of(jnp.clip(start - WINDOW, lc, t - span), LANE)
    s_pos = ws + lax.broadcasted_iota(jnp.int32, (span, cols), 0)
    t_pos = start + lax.broadcasted_iota(jnp.int32, (span, cols), 1) % qb
    allowed = jnp.abs(t_pos - s_pos) <= jnp.where(is_latent, WINDOW, -1)
    qt = jnp.stack([jnp.concatenate([qt_ref[g, i] for i in range(n_grp)], axis=1) for g in range(n_kv)], axis=0)
    sink = jnp.stack([jnp.concatenate([jnp.full((1, qb), sink_ref[g * n_grp + i], F32) for i in range(n_grp)], axis=1)
                      for g in range(n_kv)], axis=0)
    s_ctx = jnp.einsum("gkd,gdq->gkq", k_ref[:, 0:lc, :], qt, preferred_element_type=F32)
    s_win = jnp.einsum("gkd,gdq->gkq", k_ref[:, pl.ds(ws, span), :], qt, preferred_element_type=F32)
    s_win = jnp.where(allowed, s_win, NEG_BIG)
    m = jnp.maximum(jnp.maximum(jnp.max(s_ctx, axis=1, keepdims=True), jnp.max(s_win, axis=1, keepdims=True)), sink)
    p_ctx = jnp.exp(s_ctx - m)
    p_win = jnp.exp(s_win - m)
    den = jnp.sum(p_ctx, axis=1, keepdims=True) + jnp.sum(p_win, axis=1, keepdims=True) + jnp.exp(sink - m)
    o = (jnp.einsum("gdk,gkq->gdq", vt_ref[:, :, 0:lc], p_ctx.astype(BF16), preferred_element_type=F32)
         + jnp.einsum("gdk,gkq->gdq", vt_ref[:, :, pl.ds(ws, span)], p_win.astype(BF16), preferred_element_type=F32))
    o = (o / den).astype(o_ref.dtype)
    for g in range(n_kv):
        for i in range(n_grp):
            o_ref[g, i] = o[g, :, i * qb:(i + 1) * qb]


def _attention(q, k, v, sink, lc):
    n_batch, t, qw = q.shape
    hd = SWA_HEAD_DIM
    n_kv = k.shape[-1] // hd
    n_grp = qw // (n_kv * hd)
    qt = q.astype(BF16).reshape(n_batch, t, n_kv, n_grp, hd).transpose(0, 2, 3, 4, 1)
    kg = k.astype(BF16).reshape(n_batch, t, n_kv, hd).transpose(0, 2, 1, 3)
    vt = v.astype(BF16).reshape(n_batch, t, n_kv, hd).transpose(0, 2, 3, 1)
    out = pl.pallas_call(
        functools.partial(_attn_kernel, lc=lc),
        grid=(n_batch, t // ATTN_BLOCK),
        in_specs=[
            pl.BlockSpec(memory_space=pltpu.SMEM),
            pl.BlockSpec((None, n_kv, n_grp, hd, ATTN_BLOCK), lambda b, j: (b, 0, 0, 0, j)),
            pl.BlockSpec((None, n_kv, t, hd), lambda b, j: (b, 0, 0, 0)),
            pl.BlockSpec((None, n_kv, hd, t), lambda b, j: (b, 0, 0, 0)),
        ],
        out_specs=pl.BlockSpec((None, n_kv, n_grp, hd, ATTN_BLOCK), lambda b, j: (b, 0, 0, 0, j)),
        out_shape=jax.ShapeDtypeStruct((n_batch, n_kv, n_grp, hd, t), BF16),
        compiler_params=_params("parallel", "parallel"),
        name="window_attention",
    )(sink, qt, kg, vt)
    return out.transpose(0, 4, 1, 2, 3).reshape(n_batch, t, qw)


def _swa_tables(lc, s):
    rows = jnp.repeat(jnp.arange(s // GRID_W), GRID_W).astype(F32)
    cols = jnp.tile(jnp.arange(GRID_W), s // GRID_W).astype(F32)
    n_freq = SWA_HEAD_DIM // 4
    inv = ROPE_BASE ** (-jnp.arange(n_freq, dtype=F32) / n_freq)
    ang = jnp.concatenate([rows[:, None] * inv[None, :], cols[:, None] * inv[None, :]], axis=-1)
    cos, sin = jnp.cos(ang), jnp.sin(ang)
    reps = LANE // SWA_HEAD_DIM
    cos_t = jnp.tile(jnp.concatenate([cos, cos], axis=-1), (1, reps))
    sin_t = jnp.tile(jnp.concatenate([-sin, sin], axis=-1), (1, reps))
    cos_t = jnp.concatenate([jnp.ones((lc, LANE), F32), cos_t], axis=0)
    sin_t = jnp.concatenate([jnp.zeros((lc, LANE), F32), sin_t], axis=0)
    return cos_t, sin_t


def _window_attention_mixer(h, mod, gain, w_qkv, q_gain, k_gain, sink, w_out, lc):
    n_batch, t, d = h.shape
    n_heads = sink.shape[0]
    q_w = n_heads * SWA_HEAD_DIM
    kv_w = (w_qkv.shape[1] - q_w) // 2
    cos_t, sin_t = _swa_tables(lc, t - lc)
    w_bf = w_qkv.astype(BF16)
    q = _proj(h, mod, gain, w_bf[:, :q_w], lc)
    k = _proj(h, mod, gain, w_bf[:, q_w:q_w + kv_w], lc)
    v = _proj(h, mod, gain, w_bf[:, q_w + kv_w:], lc)
    q = _headnorm_rope(q, q_gain, cos_t, sin_t, SWA_HEAD_DIM ** -0.5, lc)
    k = _headnorm_rope(k, k_gain, cos_t, sin_t, 1.0, lc)
    y = _attention(q, k, v, sink, lc)
    return ("proj", y, w_out.astype(BF16), mod)


def _rope_halves(x, cos, sin):
    half = x.shape[1] // 2
    x1, x2 = x[:, :half], x[:, half:]
    return jnp.concatenate([x1 * cos - x2 * sin, x1 * sin + x2 * cos], axis=1)


def _ret_kernel(qf_ref, kf_ref, vf_ref, cosf_ref, sinf_ref, qb_ref, kb_ref, vb_ref, cosb_ref, sinb_ref,
                of_ref, ob_ref, sf_ref, sb_ref):
    c = RET_CHUNK
    rows = qf_ref.shape[0]
    dk = qf_ref.shape[1]
    head = pl.program_id(1)

    @pl.when(pl.program_id(2) == 0)
    def _():
        sf_ref[...] = jnp.zeros_like(sf_ref)
        sb_ref[...] = jnp.zeros_like(sb_ref)

    hv = jnp.zeros((1, 1), F32) + head.astype(F32)
    lg = jnp.log(1.0 - jnp.exp((-5.0 - hv) * math.log(2.0)))
    ri = lax.broadcasted_iota(jnp.int32, (c, c), 0)
    ci = lax.broadcasted_iota(jnp.int32, (c, c), 1)
    pos = lax.broadcasted_iota(jnp.int32, (c, 1), 0).astype(F32)
    chunk_decay = jnp.exp(lg * float(c))

    def run(q_ref, k_ref, v_ref, cos_ref, sin_ref, o_ref, s_ref, reverse):
        rel = ((ci - ri) if reverse else (ri - ci))
        decay = jnp.exp(jnp.where(rel >= 0, lg * rel.astype(F32), NEG_BIG))
        p = (c - 1.0 - pos) if reverse else pos
        q_fac = jnp.exp(lg * (p + 1.0))
        k_fac = jnp.exp(lg * (c - 1.0 - p))
        n_sub = rows // c
        order = range(n_sub - 1, -1, -1) if reverse else range(n_sub)
        state = s_ref[...]
        for i in order:
            sl = slice(i * c, (i + 1) * c)
            q = _rope_halves(q_ref[sl, :].astype(F32), cos_ref[sl, :], sin_ref[sl, :])
            k = _rope_halves(k_ref[sl, :].astype(F32), cos_ref[sl, :], sin_ref[sl, :]) * (dk ** -0.5)
            v = v_ref[sl, :].astype(BF16)
            scores = _dot_nt(q.astype(BF16), k.astype(BF16)) * decay
            o = _dot(scores.astype(BF16), v) + _dot((q * q_fac).astype(BF16), state.astype(BF16))
            o_ref[sl, :] = o.astype(o_ref.dtype)
            state = state * chunk_decay + _dot_tn((k * k_fac).astype(BF16), v)
        s_ref[...] = state

    run(qf_ref, kf_ref, vf_ref, cosf_ref, sinf_ref, of_ref, sf_ref, False)
    run(qb_ref, kb_ref, vb_ref, cosb_ref, sinb_ref, ob_ref, sb_ref, True)


def _retention_core(proj, cos_t, sin_t, lc):
    n_batch, t, width = proj.shape
    nh = RET_HEADS
    dk = width // (8 * nh)
    dv = 2 * dk
    rows = lc
    n_steps = t // rows
    n_ctx = lc // rows

    def fwd(b, h, s):
        return s

    def bwd(b, h, s):
        return jnp.where(s < n_ctx, n_ctx - 1 - s, n_steps - 1 - s + n_ctx)

    def specs(step):
        return [
            pl.BlockSpec((None, rows, dk), lambda b, h, s: (b, step(b, h, s), h)),
            pl.BlockSpec((None, rows, dk), lambda b, h, s: (b, step(b, h, s), nh + h)),
            pl.BlockSpec((None, rows, dv), lambda b, h, s: (b, step(b, h, s), nh + h)),
            pl.BlockSpec((rows, dk // 2), lambda b, h, s: (step(b, h, s), 0)),
            pl.BlockSpec((rows, dk // 2), lambda b, h, s: (step(b, h, s), 0)),
        ]

    out_sds = jax.ShapeDtypeStruct((n_batch, t, nh * dv), BF16)
    return pl.pallas_call(
        _ret_kernel,
        grid=(n_batch, nh, n_steps),
        in_specs=specs(fwd) + specs(bwd),
        out_specs=[
            pl.BlockSpec((None, rows, dv), lambda b, h, s: (b, fwd(b, h, s), h)),
            pl.BlockSpec((None, rows, dv), lambda b, h, s: (b, bwd(b, h, s), h)),
        ],
        out_shape=[out_sds, out_sds],
        scratch_shapes=[pltpu.VMEM((dk, dv), F32), pltpu.VMEM((dk, dv), F32)],
        compiler_params=_params("parallel", "parallel", "arbitrary"),
        name="retention",
    )(proj, proj, proj, cos_t, sin_t, proj, proj, proj, cos_t, sin_t)


def _retention_finish(of_ref, ob_ref, gf_ref, gb_ref, gain_ref, n_heads):
    dv = of_ref.shape[1] // n_heads
    parts = []
    for hh in range(n_heads):
        sl = slice(hh * dv, (hh + 1) * dv)

        def group_norm(o, gain):
            mu = jnp.mean(o, axis=-1, keepdims=True)
            xc = o - mu
            var = jnp.mean(xc * xc, axis=-1, keepdims=True)
            return xc * lax.rsqrt(var + EPS) * gain

        y = (_silu(gf_ref[:, sl].astype(F32)) * group_norm(of_ref[:, sl].astype(F32), gain_ref[0:1, sl])
             + _silu(gb_ref[:, sl].astype(F32)) * group_norm(ob_ref[:, sl].astype(F32), gain_ref[1:2, sl]))
        parts.append(y.astype(BF16))
    return jnp.concatenate(parts, axis=1)


def _ret_tables(lc, s, n_freq):
    inv = ROPE_BASE ** (-jnp.arange(n_freq, dtype=F32) / n_freq)
    ang = jnp.arange(s, dtype=F32)[:, None] * inv[None, :]
    cos_t = jnp.concatenate([jnp.ones((lc, n_freq), F32), jnp.cos(ang)], axis=0)
    sin_t = jnp.concatenate([jnp.zeros((lc, n_freq), F32), jnp.sin(ang)], axis=0)
    return cos_t, sin_t


def _retention_mixer(h, mod, gain, w_in, gn_gain, w_out, lc):
    n_batch, t, d = h.shape
    dk = w_in.shape[1] // (8 * RET_HEADS)
    cos_t, sin_t = _ret_tables(lc, t - lc, dk // 2)
    proj = _proj(h, mod, gain, w_in.astype(BF16), lc, out_dtype=BF16)
    o_f, o_b = _retention_core(proj, cos_t, sin_t, lc)
    return ("retention", o_f, o_b, proj, gn_gain, w_out.astype(BF16), mod)


def kernel(x, c, ctx, c_ctx, ada_w, ada_b, norm_g, ffn_w1, ffn_w2, dn_w_in, dn_conv, dn_a_log, dn_dt_bias,
           dn_o_gain, dn_w_out, swa_w_qkv, swa_q_gain, swa_k_gain, swa_sink, swa_w_out, ret_w_in, ret_gn_gain,
           ret_w_out):
    n_batch, s, d = x.shape
    lc = ctx.shape[1]
    depth = ada_w.shape[0]
    n_mod = ada_w.shape[2] // d

    bp = -(-(n_batch + 1) // SUBLANE) * SUBLANE
    cond = jnp.concatenate([c, c_ctx[None, :], jnp.zeros((bp - n_batch - 1, d), F32)], axis=0)
    mods = _ada_mods(cond, ada_w, ada_b).reshape(depth, bp, n_mod, d)

    w1 = ffn_w1.astype(BF16)
    w2 = ffn_w2.astype(BF16)
    h = None
    for i in range(depth):
        kind, slot = i % N_MIXERS, i // N_MIXERS
        sub = [mods[i, :, 3 * j:3 * j + 3, :] for j in range(3)]
        if i == 0:
            h = _ffn_joining(ctx, x, sub[0], norm_g[i, 0], w1[i, 0], w2[i, 0])
        else:
            h = _ffn(h, sub[0], norm_g[i, 0], w1[i, 0], w2[i, 0], lc)
        if kind == 0:
            mixed = _deltanet_mixer(h, sub[1], norm_g[i, 1], dn_w_in[slot], dn_conv[slot], dn_a_log[slot],
                                    dn_dt_bias[slot], dn_o_gain[slot], dn_w_out[slot], lc)
        elif kind == 1:
            mixed = _window_attention_mixer(h, sub[1], norm_g[i, 1], swa_w_qkv[slot], swa_q_gain[slot],
                                            swa_k_gain[slot], swa_sink[slot], swa_w_out[slot], lc)
        else:
            mixed = _retention_mixer(h, sub[1], norm_g[i, 1], ret_w_in[slot], ret_gn_gain[slot], ret_w_out[slot],
                                     lc)
        h = _ffn(h, sub[2], norm_g[i, 2], w1[i, 1], w2[i, 1], lc, latent_only=(i == depth - 1), pending=mixed)
    return h
```

```python
import functools
import math

import jax
import jax.numpy as jnp
from jax import lax
from jax.experimental import pallas as pl
from jax.experimental.pallas import tpu as pltpu

F32 = jnp.float32
BF16 = jnp.bfloat16

N_MIXERS = 3
GRID_W = 64
ROPE_BASE = 10000.0
EPS = 1e-6
FFN_RESIDUAL = 0.5
DN_HEAD_DIM = 128
DN_CHUNK = 64
DN_INV_BLOCK = 16
DN_CHUNK_BATCHES = (18, 12, 6, 4, 2)
SWA_HEAD_DIM = 64
SWA_GROUP = 4
WINDOW = 128
ATTN_BLOCK = 128
RET_HEADS = 4
RET_CHUNK = 128

LANE = 128
SUBLANE = 8
VMEM_LIMIT_BYTES = 52 * 1024 * 1024
NEG_BIG = -1e30


def _params(*semantics):
    return pltpu.CompilerParams(dimension_semantics=semantics, vmem_limit_bytes=VMEM_LIMIT_BYTES)


def _resident(block_shape, index_map):
    return pl.BlockSpec(block_shape, index_map, pipeline_mode=pl.Buffered(1))


def _dot(a, b):
    return jnp.dot(a, b, preferred_element_type=F32)


def _dot_nt(a, b):
    return lax.dot_general(a, b, (((1,), (1,)), ((), ())), preferred_element_type=F32)


def _dot_tn(a, b):
    return lax.dot_general(a, b, (((0,), (0,)), ((), ())), preferred_element_type=F32)


def _split2(x):
    hi = x.astype(BF16)
    lo = (x - hi.astype(F32)).astype(BF16)
    return hi, lo


def _split3(x):
    hi = x.astype(BF16)
    r = x - hi.astype(F32)
    mid = r.astype(BF16)
    lo = (r - mid.astype(F32)).astype(BF16)
    return hi, mid, lo


def _dot_x3(a, b):
    ah, al = _split2(a)
    bh, bl = _split2(b)
    return _dot(ah, bh) + (_dot(ah, bl) + _dot(al, bh))


def _sigmoid(x):
    return 1.0 / (1.0 + jnp.exp(-x))


def _silu(x):
    return x * _sigmoid(x)


def _softplus(x):
    return jnp.maximum(x, 0.0) + jnp.log(1.0 + jnp.exp(-jnp.abs(x)))


def _mod_norm(x, gain, shift, scale):
    ms = jnp.mean(x * x, axis=-1, keepdims=True)
    return (x * lax.rsqrt(ms + EPS)) * (gain * (1.0 + scale)) + shift


def _ada_kernel(c_ref, w_ref, b_ref, o_ref):
    o_ref[...] = _dot_x3(_silu(c_ref[...]), w_ref[...]) + b_ref[...]


def _ada_mods(cond, ada_w, ada_b):
    depth, d, nd = ada_w.shape
    bp = cond.shape[0]
    tn = d
    return pl.pallas_call(
        _ada_kernel,
        grid=(depth, nd // tn),
        in_specs=[
            pl.BlockSpec((bp, d), lambda l, n: (0, 0)),
            pl.BlockSpec((None, d, tn), lambda l, n: (l, 0, n)),
            pl.BlockSpec((None, 1, tn), lambda l, n: (l, 0, n)),
        ],
        out_specs=pl.BlockSpec((None, bp, tn), lambda l, n: (l, 0, n)),
        out_shape=jax.ShapeDtypeStruct((depth, bp, nd), F32),
        compiler_params=_params("parallel", "parallel"),
        name="ada_mods",
    )(cond, ada_w, ada_b.reshape(depth, 1, nd))


def _row_tile(lc, s):
    tm = 256
    while lc % tm or s % tm:
        tm //= 2
    return tm


def _mod_index(n_batch, n_ctx_tiles):
    return lambda b, t: (jnp.where(t < n_ctx_tiles, n_batch, b), 0, 0)


def _ffn_rows(x, mod_ref, g_ref, w1_ref, w2_ref, o_ref):
    mod = mod_ref[...]
    hn = _mod_norm(x, g_ref[...], mod[0:1], mod[1:2]).astype(BF16)
    f = w2_ref.shape[0]
    a = _dot(hn, w1_ref[:, :f])
    b = _dot(hn, w1_ref[:, f:])
    y = _dot((_silu(a) * b).astype(BF16), w2_ref[...])
    o_ref[...] = x + (FFN_RESIDUAL * mod[2:3]) * y


def _ffn_kernel(x_ref, mod_ref, g_ref, w1_ref, w2_ref, o_ref):
    _ffn_rows(x_ref[...], mod_ref, g_ref, w1_ref, w2_ref, o_ref)


def _ffn_joining_kernel(ctx_ref, x_ref, mod_ref, g_ref, w1_ref, w2_ref, o_ref, *, n_ctx_tiles):
    tile = pl.program_id(1) + jnp.zeros(x_ref.shape, jnp.int32)
    _ffn_rows(jnp.where(tile < n_ctx_tiles, ctx_ref[...], x_ref[...]), mod_ref, g_ref, w1_ref, w2_ref, o_ref)


def _ffn_after_proj_kernel(x_ref, y_ref, wo_ref, mix_mod_ref, mod_ref, g_ref, w1_ref, w2_ref, o_ref):
    x = x_ref[...] + mix_mod_ref[2:3, :] * _dot(y_ref[...].astype(BF16), wo_ref[...])
    _ffn_rows(x, mod_ref, g_ref, w1_ref, w2_ref, o_ref)


def _ffn_after_retention_kernel(x_ref, of_ref, ob_ref, gf_ref, gb_ref, gain_ref, wo_ref, mix_mod_ref,
                                mod_ref, g_ref, w1_ref, w2_ref, o_ref, *, n_heads):
    y = _retention_finish(of_ref, ob_ref, gf_ref, gb_ref, gain_ref, n_heads)
    x = x_ref[...] + mix_mod_ref[2:3, :] * _dot(y, wo_ref[...])
    _ffn_rows(x, mod_ref, g_ref, w1_ref, w2_ref, o_ref)


def _ffn(h, mod, gain, w1, w2, lc, latent_only=False, pending=None):
    n_batch, t, d = h.shape
    f = w2.shape[0]
    tm = _row_tile(lc, t - lc)
    skip = lc // tm if latent_only else 0
    n_tiles = t // tm - skip

    def rows(width, col=0):
        return pl.BlockSpec((None, tm, width), lambda b, i: (b, i + skip, col))

    def whole(shape):
        return _resident(shape, lambda b, i: (0, 0))

    mod_spec = pl.BlockSpec((None, 3, d), _mod_index(n_batch, lc // tm - skip))
    if pending is None:
        body, specs, args = _ffn_kernel, [rows(d)], [h]
    elif pending[0] == "proj":
        _, y, w_out, mix_mod = pending
        k = y.shape[-1]
        body = _ffn_after_proj_kernel
        specs = [rows(d), rows(k), whole((k, d)), mod_spec]
        args = [h, y, w_out, mix_mod]
    else:
        _, o_f, o_b, proj, gn_gain, w_out, mix_mod = pending
        vw = o_f.shape[-1]
        gate_f = proj.shape[-1] // vw - 2
        body = functools.partial(_ffn_after_retention_kernel, n_heads=RET_HEADS)
        specs = [rows(d), rows(vw), rows(vw), rows(vw, gate_f), rows(vw, gate_f + 1),
                 pl.BlockSpec((2, vw), lambda b, i: (0, 0)), whole((vw, d)), mod_spec]
        args = [h, o_f, o_b, proj, proj, gn_gain, w_out, mix_mod]
    return pl.pallas_call(
        body,
        grid=(n_batch, n_tiles),
        in_specs=specs + [mod_spec, pl.BlockSpec((1, d), lambda b, i: (0, 0)), whole((d, 2 * f)), whole((f, d))],
        out_specs=pl.BlockSpec((None, tm, d), lambda b, i: (b, i, 0)),
        out_shape=jax.ShapeDtypeStruct((n_batch, n_tiles * tm, d), F32),
        compiler_params=_params("parallel", "parallel"),
        name="ffn",
    )(*args, mod, gain.reshape(1, d), w1, w2)


def _ffn_joining(ctx, x, mod, gain, w1, w2):
    n_batch, s, d = x.shape
    lc = ctx.shape[1]
    f = w2.shape[0]
    tm = _row_tile(lc, s)
    nct = lc // tm
    return pl.pallas_call(
        functools.partial(_ffn_joining_kernel, n_ctx_tiles=nct),
        grid=(n_batch, (lc + s) // tm),
        in_specs=[
            pl.BlockSpec((None, tm, d), lambda b, i: (b, jnp.minimum(i, nct - 1), 0)),
            pl.BlockSpec((None, tm, d), lambda b, i: (b, jnp.maximum(i - nct, 0), 0)),
            pl.BlockSpec((None, 3, d), _mod_index(n_batch, nct)),
            pl.BlockSpec((1, d), lambda b, i: (0, 0)),
            _resident((d, 2 * f), lambda b, i: (0, 0)),
            _resident((f, d), lambda b, i: (0, 0)),
        ],
        out_specs=pl.BlockSpec((None, tm, d), lambda b, i: (b, i, 0)),
        out_shape=jax.ShapeDtypeStruct((n_batch, lc + s, d), F32),
        compiler_params=_params("parallel", "parallel"),
        name="ffn_joining",
    )(ctx, x, mod, gain.reshape(1, d), w1, w2)


def _proj_kernel(x_ref, mod_ref, g_ref, w_ref, o_ref):
    mod = mod_ref[...]
    hn = _mod_norm(x_ref[...], g_ref[...], mod[0:1], mod[1:2]).astype(BF16)
    o_ref[...] = _dot(hn, w_ref[...]).astype(o_ref.dtype)


PROJ_WEIGHT_TILE_BYTES = 16 * 1024 * 1024


def _col_tile(w):
    d, n = w.shape
    if w.size * w.dtype.itemsize <= PROJ_WEIGHT_TILE_BYTES:
        return n
    tn = 2048
    while n % tn:
        tn -= LANE
    return tn


def _proj(h, mod, gain, w, lc, out_dtype=F32):
    n_batch, t, d = h.shape
    n = w.shape[1]
    tm = _row_tile(lc, t - lc)
    tn = _col_tile(w)
    mod_idx = _mod_index(n_batch, lc // tm)
    w_spec = (_resident if tn == n else pl.BlockSpec)((d, tn), lambda j, b, i: (0, j))
    return pl.pallas_call(
        _proj_kernel,
        grid=(n // tn, n_batch, t // tm),
        in_specs=[
            pl.BlockSpec((None, tm, d), lambda j, b, i: (b, i, 0)),
            pl.BlockSpec((None, 3, d), lambda j, b, i: mod_idx(b, i)),
            pl.BlockSpec((1, d), lambda j, b, i: (0, 0)),
            w_spec,
        ],
        out_specs=pl.BlockSpec((None, tm, tn), lambda j, b, i: (b, i, j)),
        out_shape=jax.ShapeDtypeStruct((n_batch, t, n), out_dtype),
        compiler_params=_params("parallel", "parallel", "parallel"),
        name="mixer_in",
    )(h, mod, gain.reshape(1, d), w)


def _proj_split_kernel(x_ref, mod_ref, g_ref, w_ref, *o_refs):
    mod = mod_ref[...]
    hn = _mod_norm(x_ref[...], g_ref[...], mod[0:1], mod[1:2]).astype(BF16)
    y = _dot(hn, w_ref[...])
    col = 0
    for o_ref in o_refs:
        o_ref[...] = y[:, col:col + o_ref.shape[1]]
        col += o_ref.shape[1]


def _proj_split(h, mod, gain, w, lc, widths):
    n_batch, t, d = h.shape
    n = w.shape[1]
    tm = _row_tile(lc, t - lc)
    return pl.pallas_call(
        _proj_split_kernel,
        grid=(n_batch, t // tm),
        in_specs=[
            pl.BlockSpec((None, tm, d), lambda b, i: (b, i, 0)),
            pl.BlockSpec((None, 3, d), _mod_index(n_batch, lc // tm)),
            pl.BlockSpec((1, d), lambda b, i: (0, 0)),
            _resident((d, n), lambda b, i: (0, 0)),
        ],
        out_specs=[pl.BlockSpec((None, tm, wd), lambda b, i: (b, i, 0)) for wd in widths],
        out_shape=[jax.ShapeDtypeStruct((n_batch, t, wd), F32) for wd in widths],
        compiler_params=_params("parallel", "parallel"),
        name="mixer_in_split",
    )(h, mod, gain.reshape(1, d), w)


def _bmm(a, b):
    return jnp.einsum("gik,gkj->gij", a.astype(BF16), b.astype(BF16), preferred_element_type=F32)


def _pair_mm(x, y, left):
    y = y.astype(BF16)
    zero = jnp.zeros_like(y)
    y_diag = jnp.concatenate([jnp.where(left, y, zero), jnp.where(left, zero, y)], axis=1)
    return jnp.einsum("gik,gkj->gij", x.astype(BF16), y_diag, preferred_element_type=F32)


def _unit_tri_inverse(a, blk_mask, eye_f, left):
    c = a.shape[1]
    d = jnp.where(blk_mask, a, 0.0)
    e = a - d
    p = eye_f - d
    dk = d
    for _ in range(DN_INV_BLOCK.bit_length() - 2):
        dk = _pair_mm(dk, dk, left)
        p = _pair_mm(p, eye_f + dk, left)
    n = _pair_mm(p, e, left)
    x = eye_f - n
    nk = n
    for _ in range((c // DN_INV_BLOCK).bit_length() - 2):
        nk = _pair_mm(nk, nk, left)
        x = _pair_mm(x, eye_f + nk, left)
    return _pair_mm(x, p, left)


def _dn_kernel(q_ref, k_ref, v_ref, z_ref, cwq_ref, cwk_ref, cwv_ref, gcol_ref, grow_ref, og_ref,
               y_ref,
               pad_s, q_s, k_s, v_s, kq_s, b_s, a_s, o_s, *, lc):
    t = q_ref.shape[0]
    c = DN_CHUNK
    nc = t // c
    ncc = lc // c
    rb = _row_tile(lc, t - lc)
    hd = DN_HEAD_DIM

    zeros8 = jnp.zeros((SUBLANE, DN_HEAD_DIM), F32)

    def conv_act(x_ref, cw_ref, dst, normalise):
        pad_s[0:SUBLANE, :] = zeros8
        pad_s[SUBLANE:SUBLANE + lc, :] = x_ref[0:lc, :]
        pad_s[SUBLANE + lc:2 * SUBLANE + lc, :] = zeros8
        pad_s[2 * SUBLANE + lc:2 * SUBLANE + t, :] = x_ref[lc:t, :]
        pad_s[2 * SUBLANE + t:3 * SUBLANE + t, :] = zeros8
        cw = cw_ref[...]
        n_tap = cw.shape[0]
        for r0 in range(0, t, rb):
            base = r0 + (SUBLANE if r0 < lc else 2 * SUBLANE)
            acc = None
            for j in range(n_tap):
                lo = base + j - n_tap // 2
                term = pad_s[lo:lo + rb, :] * cw[j:j + 1, :]
                acc = term if acc is None else acc + term
            act = _silu(acc)
            if normalise:
                act = act * lax.rsqrt(jnp.sum(act * act, axis=-1, keepdims=True) + EPS)
            dst[r0:r0 + rb, :] = act

    conv_act(q_ref, cwq_ref, q_s, True)
    conv_act(k_ref, cwk_ref, k_s, True)
    conv_act(v_ref, cwv_ref, v_s, False)

    ri = lax.broadcasted_iota(jnp.int32, (c, 2 * c), 0)
    li = lax.broadcasted_iota(jnp.int32, (c, 2 * c), 1)
    left = li < c
    ci = jnp.where(left, li, li - c)
    eye_f = jnp.where(ri == ci, 1.0, 0.0).astype(F32)
    blk_mask = (ri // DN_INV_BLOCK) == (ci // DN_INV_BLOCK)
    lag = jnp.where(left, ri - ci, ci - ri)
    incl = lag >= 0
    strict = lag > 0
    q_scale = DN_HEAD_DIM ** -0.5
    g = max(n for n in DN_CHUNK_BATCHES if nc % n == 0)

    def phase1(it, carry):
        ch0 = it * g
        rows = pl.ds(pl.multiple_of(ch0 * c, c), g * c)
        q = q_s[rows, :].reshape(g, c, DN_HEAD_DIM) * q_scale
        k = k_s[rows, :].reshape(g, c, DN_HEAD_DIM)
        v = v_s[rows, :].reshape(g, c, DN_HEAD_DIM)
        gcol = gcol_ref[rows, :].reshape(g, c, 4)
        grow = grow_ref[pl.ds(ch0, g)]
        k_bf = k.astype(BF16)
        gcs = (gcol[:, :, 0:1], gcol[:, :, 2:3])
        kbs = (k * gcol[:, :, 1:2], k * gcol[:, :, 3:4])
        decay = jnp.exp(jnp.where(incl, jnp.where(left, gcs[0], gcs[1]) - grow, NEG_BIG))
        kq = jnp.einsum("gic,gjc->gij", jnp.concatenate([kbs[0], kbs[1], q], axis=1).astype(BF16),
                        jnp.concatenate([k_bf, k_bf], axis=1), preferred_element_type=F32)
        a = jnp.where(strict, jnp.where(left, kq[:, :c], kq[:, c:2 * c]) * decay, 0.0)
        qk = (kq[:, 2 * c:] * decay).astype(BF16)
        tinv = _unit_tri_inverse(a, blk_mask, eye_f, left).astype(BF16)
        for d in range(2):
            gc, kb, beta = gcs[d], kbs[d], gcol[:, :, 2 * d + 1:2 * d + 2]

            def own_rows(x):
                zero = jnp.zeros_like(x)
                return jnp.concatenate([x, zero] if d == 0 else [zero, x], axis=1)

            eg = jnp.exp(gc)
            wu = jnp.einsum("gik,gkj->gij", tinv, own_rows(jnp.concatenate([kb * eg, v * beta], axis=2).astype(BF16)),
                            preferred_element_type=F32).astype(BF16)
            g_tot = gc[:, c - 1:c, :] if d == 0 else gc[:, 0:1, :]
            kd = (k * jnp.exp(g_tot - gc)).astype(BF16)
            kwu = jnp.einsum("gck,gcn->gkn", kd, wu, preferred_element_type=F32)
            qwu = jnp.einsum("gik,gkj->gij", qk, own_rows(wu), preferred_element_type=F32)
            qp = q * eg - qwu[:, :, :hd]
            ob = qwu[:, :, hd:]
            dec = jnp.exp(g_tot)
            first, second = (0, 1) if d == 0 else (1, 0)

            def of_pair(x, which):
                return x.reshape((g // 2, 2) + x.shape[1:])[:, which]

            kb1, kb2 = of_pair(kwu, first), of_pair(kwu, second)
            q1, q2 = of_pair(qp, first), of_pair(qp, second)
            a1, a2 = of_pair(dec, first), of_pair(dec, second)
            cross = _bmm(jnp.concatenate([kb2[:, :, :hd], q2], axis=1), kb1)
            k_pair = a2 * kb1[:, :, :hd] + a1 * kb2[:, :, :hd] - cross[:, :hd, :hd]
            b_pair = a2 * kb1[:, :, hd:] + kb2[:, :, hd:] - cross[:, :hd, hd:]
            q2 = a1 * q2 - cross[:, hd:, :hd]
            ob1, ob2 = of_pair(ob, first), of_pair(ob, second) + cross[:, hd:, hd:]
            q_rows, ob_rows = ((q1, q2), (ob1, ob2)) if d == 0 else ((q2, q1), (ob2, ob1))
            pairs = pl.ds(it * (g // 2), g // 2)
            kq_s[d, pairs, 0:hd, :] = k_pair.astype(BF16)
            kq_s[d, pairs, hd:hd + c, :] = q_rows[0].astype(BF16)
            kq_s[d, pairs, hd + c:, :] = q_rows[1].astype(BF16)
            b_s[d, pairs] = b_pair
            o_s[d, rows, :] = jnp.concatenate(ob_rows, axis=1).reshape(g * c, hd)
            a_s[d, pairs] = jnp.broadcast_to(a1 * a2, (g // 2, SUBLANE, hd))
        return carry

    lax.fori_loop(0, nc // g, phase1, 0)

    n_pairs, n_ctx_pairs = nc // 2, ncc // 2

    def phase2(s, carry):
        pair_of = (s, jnp.where(s < n_ctx_pairs, n_ctx_pairs - 1 - s, n_pairs - 1 - s + n_ctx_pairs))
        new = []
        for d in range(2):
            state = carry[d]
            pr = pair_of[d]
            rows = pl.ds(pl.multiple_of(pr * 2 * c, 2 * c), 2 * c)
            r = _dot(kq_s[d, pr], state.astype(BF16))
            o_s[d, rows, :] = o_s[d, rows, :] + r[hd:]
            new.append(state * a_s[d, pr][0:1, :] + (b_s[d, pr] - r[:hd]))
        return tuple(new)

    zero_state = jnp.zeros((hd, hd), F32)
    lax.fori_loop(0, n_pairs, phase2, (zero_state, zero_state))

    og = og_ref[...]
    for r0 in range(0, t, rb):
        o = o_s[0, r0:r0 + rb, :] + o_s[1, r0:r0 + rb, :]
        o = o * lax.rsqrt(jnp.mean(o * o, axis=-1, keepdims=True) + EPS) * og
        y_ref[r0:r0 + rb, :] = o * _silu(z_ref[r0:r0 + rb, :])


def _dn_gates_kernel(ab_ref, alog_ref, dtb_ref, o_ref, *, n_heads):
    t, width = ab_ref.shape
    c = DN_CHUNK
    col = lax.broadcasted_iota(jnp.int32, (1, width), 1)
    is_decay = (col % (2 * n_heads)) < n_heads
    backward = col >= 2 * n_heads
    ri = lax.broadcasted_iota(jnp.int32, (c, c), 0)
    ci = lax.broadcasted_iota(jnp.int32, (c, c), 1)
    tri_f = jnp.where(ci <= ri, 1.0, 0.0).astype(BF16)
    tri_b = jnp.where(ci >= ri, 1.0, 0.0).astype(BF16)
    neg_a = -jnp.exp(alog_ref[...])
    dtb = dtb_ref[...]
    for r0 in range(0, t, c):
        raw = ab_ref[r0:r0 + c, :]
        g = neg_a * _softplus(raw + dtb)
        g1, g2, g3 = _split3(g)
        cum_f = _dot(tri_f, g1) + (_dot(tri_f, g2) + _dot(tri_f, g3))
        cum_b = _dot(tri_b, g1) + (_dot(tri_b, g2) + _dot(tri_b, g3))
        o_ref[r0:r0 + c, :] = jnp.where(is_decay, jnp.where(backward, cum_b, cum_f), _sigmoid(raw))


def _dn_gates(ab, a_log, dt_bias):
    n_batch, t, width = ab.shape
    nh = a_log.shape[-1]

    def param_row(p):
        z = jnp.zeros((nh,), F32)
        row = jnp.concatenate([p[0], z, p[1], z])
        return jnp.pad(row, (0, width - 4 * nh)).reshape(1, width)

    return pl.pallas_call(
        functools.partial(_dn_gates_kernel, n_heads=nh),
        grid=(n_batch,),
        in_specs=[
            pl.BlockSpec((None, t, width), lambda b: (b, 0, 0)),
            pl.BlockSpec((1, width), lambda b: (0, 0)),
            pl.BlockSpec((1, width), lambda b: (0, 0)),
        ],
        out_specs=pl.BlockSpec((None, t, width), lambda b: (b, 0, 0)),
        out_shape=jax.ShapeDtypeStruct(ab.shape, F32),
        compiler_params=_params("parallel"),
        name="deltanet_gates",
    )(ab, param_row(a_log), param_row(dt_bias))


def _dn_core(qkvz, gate_cols, gate_rows, conv_w, o_gain, lc):
    n_batch, t, width = qkvz.shape
    hd = DN_HEAD_DIM
    nh = width // (4 * hd)
    nc = t // DN_CHUNK

    def col_block(offset):
        return pl.BlockSpec((None, t, hd), lambda b, h: (b, 0, offset + h))

    def conv_block(offset):
        return pl.BlockSpec((conv_w.shape[0], hd), lambda b, h: (0, offset + h))

    return pl.pallas_call(
        functools.partial(_dn_kernel, lc=lc),
        grid=(n_batch, nh),
        in_specs=[
            col_block(0), col_block(nh), col_block(2 * nh), col_block(3 * nh),
            conv_block(0), conv_block(nh), conv_block(2 * nh),
            pl.BlockSpec((None, None, t, 4), lambda b, h: (b, h, 0, 0)),
            pl.BlockSpec((None, None, nc, 1, 2 * DN_CHUNK), lambda b, h: (b, h, 0, 0, 0)),
            pl.BlockSpec((1, hd), lambda b, h: (0, 0)),
        ],
        out_specs=pl.BlockSpec((None, t, hd), lambda b, h: (b, 0, h)),
        out_shape=jax.ShapeDtypeStruct((n_batch, t, nh * hd), F32),
        scratch_shapes=[
            pltpu.VMEM((t + 3 * SUBLANE, hd), F32),
            pltpu.VMEM((t, hd), F32),
            pltpu.VMEM((t, hd), F32),
            pltpu.VMEM((t, hd), F32),
            pltpu.VMEM((2, nc // 2, hd + 2 * DN_CHUNK, hd), BF16),
            pltpu.VMEM((2, nc // 2, hd, hd), F32),
            pltpu.VMEM((2, nc // 2, SUBLANE, hd), F32),
            pltpu.VMEM((2, t, hd), F32),
        ],
        compiler_params=_params("parallel", "parallel"),
        name="deltanet",
    )(qkvz, qkvz, qkvz, qkvz, conv_w, conv_w, conv_w, gate_cols, gate_rows, o_gain.reshape(1, hd))


def _deltanet_mixer(h, mod, gain, w_in, conv_w, a_log, dt_bias, o_gain, w_out, lc):
    n_batch, t, d = h.shape
    nh = a_log.shape[-1]
    width = 4 * nh * DN_HEAD_DIM
    w_pad = jnp.pad(w_in, ((0, 0), (0, LANE - 4 * nh))).astype(BF16)
    qkvz, ab = _proj_split(h, mod, gain, w_pad, lc, (width, LANE))
    gates = _dn_gates(ab, a_log, dt_bias)[:, :, :4 * nh]
    gates = gates.reshape(n_batch, t, 2, 2, nh)
    gate_cols = gates.transpose(0, 4, 1, 2, 3).reshape(n_batch, nh, t, 4)
    gate_rows = (gates[:, :, :, 0, :].transpose(0, 3, 2, 1)
                 .reshape(n_batch, nh, 2, t // DN_CHUNK, DN_CHUNK).transpose(0, 1, 3, 2, 4)
                 .reshape(n_batch, nh, t // DN_CHUNK, 1, 2 * DN_CHUNK))
    y = _dn_core(qkvz, gate_cols, gate_rows, conv_w, o_gain, lc)
    return ("proj", y, w_out.astype(BF16), mod)


def _headnorm_rope_kernel(x_ref, gain_ref, cos_ref, sin_ref, e_ref, et_ref, o_ref, *, scale):
    x = x_ref[...]
    width = x.shape[1]
    sq_hi, sq_lo = _split2(x * x)
    ss = _dot(sq_hi, e_ref[...]) + _dot(sq_lo, e_ref[...])
    inv_hi, inv_lo = _split2(lax.rsqrt(ss * (1.0 / SWA_HEAD_DIM) + EPS))
    inv = _dot(inv_hi, et_ref[...]) + _dot(inv_lo, et_ref[...])
    lane = lax.broadcasted_iota(jnp.int32, (1, LANE), 1)
    first_half = (lane % SWA_HEAD_DIM) < SWA_HEAD_DIM // 2
    gain = gain_ref[...]
    cos = cos_ref[...]
    sin = sin_ref[...]
    for s in range(width // LANE):
        xs = x[:, s * LANE:(s + 1) * LANE] * inv[:, s * LANE:(s + 1) * LANE] * gain
        partner = jnp.where(first_half, pltpu.roll(xs, LANE - SWA_HEAD_DIM // 2, 1),
                            pltpu.roll(xs, SWA_HEAD_DIM // 2, 1))
        o_ref[:, s * LANE:(s + 1) * LANE] = (xs * cos + partner * sin) * scale


def _headnorm_rope(x, gain, cos_t, sin_t, scale, lc):
    n_batch, t, width = x.shape
    tm = _row_tile(lc, t - lc)
    heads = lax.broadcasted_iota(jnp.int32, (width, LANE), 0) // SWA_HEAD_DIM
    e = (heads == lax.broadcasted_iota(jnp.int32, (width, LANE), 1)).astype(BF16)
    gain_t = jnp.tile(gain, LANE // SWA_HEAD_DIM).reshape(1, LANE)
    return pl.pallas_call(
        functools.partial(_headnorm_rope_kernel, scale=scale),
        grid=(n_batch, t // tm),
        in_specs=[
            pl.BlockSpec((None, tm, width), lambda b, i: (b, i, 0)),
            pl.BlockSpec((1, LANE), lambda b, i: (0, 0)),
            pl.BlockSpec((tm, LANE), lambda b, i: (i, 0)),
            pl.BlockSpec((tm, LANE), lambda b, i: (i, 0)),
            pl.BlockSpec((width, LANE), lambda b, i: (0, 0)),
            pl.BlockSpec((LANE, width), lambda b, i: (0, 0)),
        ],
        out_specs=pl.BlockSpec((None, tm, width), lambda b, i: (b, i, 0)),
        out_shape=jax.ShapeDtypeStruct(x.shape, F32),
        compiler_params=_params("parallel", "parallel"),
        name="headnorm_rope",
    )(x, gain_t, cos_t, sin_t, e, e.T)


def _attn_kernel(sink_ref, qt_ref, k_ref, vt_ref, o_ref, *, lc):
    n_kv, n_grp, hd, qb = qt_ref.shape
    t = k_ref.shape[1]
    span = qb + 2 * WINDOW
    cols = n_grp * qb
    start = pl.program_id(1) * qb
    is_latent = start >= lc
    ws = pl.multiple_of(jnp.clip(start - WINDOW, lc, t - span), LANE)
    s_pos = ws + lax.broadcasted_iota(jnp.int32, (span, cols), 0)
    t_pos = start + lax.broadcasted_iota(jnp.int32, (span, cols), 1) % qb
    allowed = jnp.abs(t_pos - s_pos) <= jnp.where(is_latent, WINDOW, -1)
    qt = jnp.stack([jnp.concatenate([qt_ref[g, i] for i in range(n_grp)], axis=1) for g in range(n_kv)], axis=0)
    sink = jnp.stack([jnp.concatenate([jnp.full((1, qb), sink_ref[g * n_grp + i], F32) for i in range(n_grp)], axis=1)
                      for g in range(n_kv)], axis=0)
    s_ctx = jnp.einsum("gkd,gdq->gkq", k_ref[:, 0:lc, :], qt, preferred_element_type=F32)
    s_win = jnp.einsum("gkd,gdq->gkq", k_ref[:, pl.ds(ws, span), :], qt, preferred_element_type=F32)
    s_win = jnp.where(allowed, s_win, NEG_BIG)
    m = jnp.maximum(jnp.maximum(jnp.max(s_ctx, axis=1, keepdims=True), jnp.max(s_win, axis=1, keepdims=True)), sink)
    p_ctx = jnp.exp(s_ctx - m)
    p_win = jnp.exp(s_win - m)
    den = jnp.sum(p_ctx, axis=1, keepdims=True) + jnp.sum(p_win, axis=1, keepdims=True) + jnp.exp(sink - m)
    o = (jnp.einsum("gdk,gkq->gdq", vt_ref[:, :, 0:lc], p_ctx.astype(BF16), preferred_element_type=F32)
         + jnp.einsum("gdk,gkq->gdq", vt_ref[:, :, pl.ds(ws, span)], p_win.astype(BF16), preferred_element_type=F32))
    o = (o / den).astype(o_ref.dtype)
    for g in range(n_kv):
        for i in range(n_grp):
            o_ref[g, i] = o[g, :, i * qb:(i + 1) * qb]


def _attention(q, k, v, sink, lc):
    n_batch, t, qw = q.shape
    hd = SWA_HEAD_DIM
    n_kv = k.shape[-1] // hd
    n_grp = qw // (n_kv * hd)
    qt = q.astype(BF16).reshape(n_batch, t, n_kv, n_grp, hd).transpose(0, 2, 3, 4, 1)
    kg = k.astype(BF16).reshape(n_batch, t, n_kv, hd).transpose(0, 2, 1, 3)
    vt = v.astype(BF16).reshape(n_batch, t, n_kv, hd).transpose(0, 2, 3, 1)
    out = pl.pallas_call(
        functools.partial(_attn_kernel, lc=lc),
        grid=(n_batch, t // ATTN_BLOCK),
        in_specs=[
            pl.BlockSpec(memory_space=pltpu.SMEM),
            pl.BlockSpec((None, n_kv, n_grp, hd, ATTN_BLOCK), lambda b, j: (b, 0, 0, 0, j)),
            pl.BlockSpec((None, n_kv, t, hd), lambda b, j: (b, 0, 0, 0)),
            pl.BlockSpec((None, n_kv, hd, t), lambda b, j: (b, 0, 0, 0)),
        ],
        out_specs=pl.BlockSpec((None, n_kv, n_grp, hd, ATTN_BLOCK), lambda b, j: (b, 0, 0, 0, j)),
        out_shape=jax.ShapeDtypeStruct((n_batch, n_kv, n_grp, hd, t), BF16),
        compiler_params=_params("parallel", "parallel"),
        name="window_attention",
    )(sink, qt, kg, vt)
    return out.transpose(0, 4, 1, 2, 3).reshape(n_batch, t, qw)


def _swa_tables(lc, s):
    rows = jnp.repeat(jnp.arange(s // GRID_W), GRID_W).astype(F32)
    cols = jnp.tile(jnp.arange(GRID_W), s // GRID_W).astype(F32)
    n_freq = SWA_HEAD_DIM // 4
    inv = ROPE_BASE ** (-jnp.arange(n_freq, dtype=F32) / n_freq)
    ang = jnp.concatenate([rows[:, None] * inv[None, :], cols[:, None] * inv[None, :]], axis=-1)
    cos, sin = jnp.cos(ang), jnp.sin(ang)
    reps = LANE // SWA_HEAD_DIM
    cos_t = jnp.tile(jnp.concatenate([cos, cos], axis=-1), (1, reps))
    sin_t = jnp.tile(jnp.concatenate([-sin, sin], axis=-1), (1, reps))
    cos_t = jnp.concatenate([jnp.ones((lc, LANE), F32), cos_t], axis=0)
    sin_t = jnp.concatenate([jnp.zeros((lc, LANE), F32), sin_t], axis=0)
    return cos_t, sin_t


def _window_attention_mixer(h, mod, gain, w_qkv, q_gain, k_gain, sink, w_out, lc):
    n_batch, t, d = h.shape
    n_heads = sink.shape[0]
    q_w = n_heads * SWA_HEAD_DIM
    kv_w = (w_qkv.shape[1] - q_w) // 2
    cos_t, sin_t = _swa_tables(lc, t - lc)
    q, k, v = _proj_split(h, mod, gain, w_qkv.astype(BF16), lc, (q_w, kv_w, kv_w))
    q = _headnorm_rope(q, q_gain, cos_t, sin_t, SWA_HEAD_DIM ** -0.5, lc)
    k = _headnorm_rope(k, k_gain, cos_t, sin_t, 1.0, lc)
    y = _attention(q, k, v, sink, lc)
    return ("proj", y, w_out.astype(BF16), mod)


def _rope_halves(x, cos, sin):
    half = x.shape[1] // 2
    x1, x2 = x[:, :half], x[:, half:]
    return jnp.concatenate([x1 * cos - x2 * sin, x1 * sin + x2 * cos], axis=1)


def _ret_kernel(qf_ref, kf_ref, vf_ref, cosf_ref, sinf_ref, qb_ref, kb_ref, vb_ref, cosb_ref, sinb_ref,
                of_ref, ob_ref, sf_ref, sb_ref, *, n_heads):
    c = RET_CHUNK
    rows = qf_ref.shape[0]
    dk = qf_ref.shape[1] // n_heads
    dv = vf_ref.shape[1] // n_heads

    @pl.when(pl.program_id(1) == 0)
    def _():
        sf_ref[...] = jnp.zeros_like(sf_ref)
        sb_ref[...] = jnp.zeros_like(sb_ref)

    ri = lax.broadcasted_iota(jnp.int32, (c, c), 0)
    ci = lax.broadcasted_iota(jnp.int32, (c, c), 1)
    pos = lax.broadcasted_iota(jnp.int32, (c, 1), 0).astype(F32)

    def run(q_ref, k_ref, v_ref, cos_ref, sin_ref, o_ref, s_ref, reverse):
        rel = ((ci - ri) if reverse else (ri - ci))
        p = (c - 1.0 - pos) if reverse else pos
        n_sub = rows // c
        order = range(n_sub - 1, -1, -1) if reverse else range(n_sub)
        for hh in range(n_heads):
            lg = math.log1p(-2.0 ** (-5 - hh))
            decay = jnp.exp(jnp.where(rel >= 0, lg * rel.astype(F32), NEG_BIG))
            q_fac = jnp.exp(lg * (p + 1.0))
            k_fac = jnp.exp(lg * (c - 1.0 - p))
            chunk_decay = math.exp(lg * c)
            qc = slice(hh * dk, (hh + 1) * dk)
            vc = slice(hh * dv, (hh + 1) * dv)
            state = s_ref[hh]
            for i in order:
                sl = slice(i * c, (i + 1) * c)
                q = _rope_halves(q_ref[sl, qc].astype(F32), cos_ref[sl, :], sin_ref[sl, :])
                k = _rope_halves(k_ref[sl, qc].astype(F32), cos_ref[sl, :], sin_ref[sl, :]) * (dk ** -0.5)
                v = v_ref[sl, vc].astype(BF16)
                scores = _dot_nt(q.astype(BF16), k.astype(BF16)) * decay
                o = _dot(scores.astype(BF16), v) + _dot((q * q_fac).astype(BF16), state.astype(BF16))
                o_ref[sl, vc] = o.astype(o_ref.dtype)
                state = state * chunk_decay + _dot_tn((k * k_fac).astype(BF16), v)
            s_ref[hh] = state

    run(qf_ref, kf_ref, vf_ref, cosf_ref, sinf_ref, of_ref, sf_ref, False)
    run(qb_ref, kb_ref, vb_ref, cosb_ref, sinb_ref, ob_ref, sb_ref, True)


def _retention_core(proj, cos_t, sin_t, lc):
    n_batch, t, width = proj.shape
    nh = RET_HEADS
    dk = width // (8 * nh)
    dv = 2 * dk
    rows = lc
    n_steps = t // rows
    n_ctx = lc // rows

    def fwd(b, s):
        return s

    def bwd(b, s):
        return jnp.where(s < n_ctx, n_ctx - 1 - s, n_steps - 1 - s + n_ctx)

    def specs(step):
        return [
            pl.BlockSpec((None, rows, nh * dk), lambda b, s: (b, step(b, s), 0)),
            pl.BlockSpec((None, rows, nh * dk), lambda b, s: (b, step(b, s), 1)),
            pl.BlockSpec((None, rows, nh * dv), lambda b, s: (b, step(b, s), 1)),
            pl.BlockSpec((rows, dk // 2), lambda b, s: (step(b, s), 0)),
            pl.BlockSpec((rows, dk // 2), lambda b, s: (step(b, s), 0)),
        ]

    out_sds = jax.ShapeDtypeStruct((n_batch, t, nh * dv), BF16)
    return pl.pallas_call(
        functools.partial(_ret_kernel, n_heads=nh),
        grid=(n_batch, n_steps),
        in_specs=specs(fwd) + specs(bwd),
        out_specs=[
            pl.BlockSpec((None, rows, nh * dv), lambda b, s: (b, fwd(b, s), 0)),
            pl.BlockSpec((None, rows, nh * dv), lambda b, s: (b, bwd(b, s), 0)),
        ],
        out_shape=[out_sds, out_sds],
        scratch_shapes=[pltpu.VMEM((nh, dk, dv), F32), pltpu.VMEM((nh, dk, dv), F32)],
        compiler_params=_params("parallel", "arbitrary"),
        name="retention",
    )(proj, proj, proj, cos_t, sin_t, proj, proj, proj, cos_t, sin_t)


def _retention_finish(of_ref, ob_ref, gf_ref, gb_ref, gain_ref, n_heads):
    dv = of_ref.shape[1] // n_heads
    parts = []
    for hh in range(n_heads):
        sl = slice(hh * dv, (hh + 1) * dv)

        def group_norm(o, gain):
            mu = jnp.mean(o, axis=-1, keepdims=True)
            xc = o - mu
            var = jnp.mean(xc * xc, axis=-1, keepdims=True)
            return xc * lax.rsqrt(var + EPS) * gain

        y = (_silu(gf_ref[:, sl].astype(F32)) * group_norm(of_ref[:, sl].astype(F32), gain_ref[0:1, sl])
             + _silu(gb_ref[:, sl].astype(F32)) * group_norm(ob_ref[:, sl].astype(F32), gain_ref[1:2, sl]))
        parts.append(y.astype(BF16))
    return jnp.concatenate(parts, axis=1)


def _ret_tables(lc, s, n_freq):
    inv = ROPE_BASE ** (-jnp.arange(n_freq, dtype=F32) / n_freq)
    ang = jnp.arange(s, dtype=F32)[:, None] * inv[None, :]
    cos_t = jnp.concatenate([jnp.ones((lc, n_freq), F32), jnp.cos(ang)], axis=0)
    sin_t = jnp.concatenate([jnp.zeros((lc, n_freq), F32), jnp.sin(ang)], axis=0)
    return cos_t, sin_t


def _retention_mixer(h, mod, gain, w_in, gn_gain, w_out, lc):
    n_batch, t, d = h.shape
    dk = w_in.shape[1] // (8 * RET_HEADS)
    cos_t, sin_t = _ret_tables(lc, t - lc, dk // 2)
    proj = _proj(h, mod, gain, w_in.astype(BF16), lc, out_dtype=BF16)
    o_f, o_b = _retention_core(proj, cos_t, sin_t, lc)
    return ("retention", o_f, o_b, proj, gn_gain, w_out.astype(BF16), mod)


def kernel(x, c, ctx, c_ctx, ada_w, ada_b, norm_g, ffn_w1, ffn_w2, dn_w_in, dn_conv, dn_a_log, dn_dt_bias,
           dn_o_gain, dn_w_out, swa_w_qkv, swa_q_gain, swa_k_gain, swa_sink, swa_w_out, ret_w_in, ret_gn_gain,
           ret_w_out):
    n_batch, s, d = x.shape
    lc = ctx.shape[1]
    depth = ada_w.shape[0]
    n_mod = ada_w.shape[2] // d

    bp = -(-(n_batch + 1) // SUBLANE) * SUBLANE
    cond = jnp.concatenate([c, c_ctx[None, :], jnp.zeros((bp - n_batch - 1, d), F32)], axis=0)
    mods = _ada_mods(cond, ada_w, ada_b).reshape(depth, bp, n_mod, d)

    w1 = ffn_w1.astype(BF16)
    w2 = ffn_w2.astype(BF16)
    h = None
    for i in range(depth):
        kind, slot = i % N_MIXERS, i // N_MIXERS
        sub = [mods[i, :, 3 * j:3 * j + 3, :] for j in range(3)]
        if i == 0:
            h = _ffn_joining(ctx, x, sub[0], norm_g[i, 0], w1[i, 0], w2[i, 0])
        else:
            h = _ffn(h, sub[0], norm_g[i, 0], w1[i, 0], w2[i, 0], lc)
        if kind == 0:
            mixed = _deltanet_mixer(h, sub[1], norm_g[i, 1], dn_w_in[slot], dn_conv[slot], dn_a_log[slot],
                                    dn_dt_bias[slot], dn_o_gain[slot], dn_w_out[slot], lc)
        elif kind == 1:
            mixed = _window_attention_mixer(h, sub[1], norm_g[i, 1], swa_w_qkv[slot], swa_q_gain[slot],
                                            swa_k_gain[slot], swa_sink[slot], swa_w_out[slot], lc)
        else:
            mixed = _retention_mixer(h, sub[1], norm_g[i, 1], ret_w_in[slot], ret_gn_gain[slot], ret_w_out[slot],
                                     lc)
        h = _ffn(h, sub[2], norm_g[i, 2], w1[i, 1], w2[i, 1], lc, latent_only=(i == depth - 1), pending=mixed)
    return h
```

```python
import functools
import math

import jax
import jax.numpy as jnp
from jax import lax
from jax.experimental import pallas as pl
from jax.experimental.pallas import tpu as pltpu

F32 = jnp.float32
BF16 = jnp.bfloat16

N_MIXERS = 3
GRID_W = 64
ROPE_BASE = 10000.0
EPS = 1e-6
FFN_RESIDUAL = 0.5
DN_HEAD_DIM = 128
DN_CHUNK = 64
DN_INV_BLOCK = 16
DN_CHUNK_BATCHES = (18, 12, 6, 4, 2)
SWA_HEAD_DIM = 64
SWA_GROUP = 4
WINDOW = 128
ATTN_BLOCK = 128
RET_HEADS = 4
RET_CHUNK = 128

LANE = 128
SUBLANE = 8
VMEM_LIMIT_BYTES = 52 * 1024 * 1024
NEG_BIG = -1e30


def _params(*semantics):
    return pltpu.CompilerParams(dimension_semantics=semantics, vmem_limit_bytes=VMEM_LIMIT_BYTES)


def _resident(block_shape, index_map):
    return pl.BlockSpec(block_shape, index_map, pipeline_mode=pl.Buffered(1))


def _dot(a, b):
    return jnp.dot(a, b, preferred_element_type=F32)


def _dot_nt(a, b):
    return lax.dot_general(a, b, (((1,), (1,)), ((), ())), preferred_element_type=F32)


def _dot_tn(a, b):
    return lax.dot_general(a, b, (((0,), (0,)), ((), ())), preferred_element_type=F32)


def _split2(x):
    hi = x.astype(BF16)
    lo = (x - hi.astype(F32)).astype(BF16)
    return hi, lo


def _split3(x):
    hi = x.astype(BF16)
    r = x - hi.astype(F32)
    mid = r.astype(BF16)
    lo = (r - mid.astype(F32)).astype(BF16)
    return hi, mid, lo


def _dot_x3(a, b):
    ah, al = _split2(a)
    bh, bl = _split2(b)
    return _dot(ah, bh) + (_dot(ah, bl) + _dot(al, bh))


def _sigmoid(x):
    return 1.0 / (1.0 + jnp.exp(-x))


def _silu(x):
    return x * _sigmoid(x)


def _softplus(x):
    return jnp.maximum(x, 0.0) + jnp.log(1.0 + jnp.exp(-jnp.abs(x)))


def _mod_norm(x, gain, shift, scale):
    ms = jnp.mean(x * x, axis=-1, keepdims=True)
    return (x * lax.rsqrt(ms + EPS)) * (gain * (1.0 + scale)) + shift


def _ada_kernel(c_ref, w_ref, b_ref, o_ref):
    o_ref[...] = _dot_x3(_silu(c_ref[...]), w_ref[...]) + b_ref[...]


def _ada_mods(cond, ada_w, ada_b):
    depth, d, nd = ada_w.shape
    bp = cond.shape[0]
    tn = d
    return pl.pallas_call(
        _ada_kernel,
        grid=(depth, nd // tn),
        in_specs=[
            pl.BlockSpec((bp, d), lambda l, n: (0, 0)),
            pl.BlockSpec((None, d, tn), lambda l, n: (l, 0, n)),
            pl.BlockSpec((None, 1, tn), lambda l, n: (l, 0, n)),
        ],
        out_specs=pl.BlockSpec((None, bp, tn), lambda l, n: (l, 0, n)),
        out_shape=jax.ShapeDtypeStruct((depth, bp, nd), F32),
        compiler_params=_params("parallel", "parallel"),
        name="ada_mods",
    )(cond, ada_w, ada_b.reshape(depth, 1, nd))


def _row_tile(lc, s):
    tm = 256
    while lc % tm or s % tm:
        tm //= 2
    return tm


def _mod_index(n_batch, n_ctx_tiles):
    return lambda b, t: (jnp.where(t < n_ctx_tiles, n_batch, b), 0, 0)


def _ffn_rows(x, mod_ref, g_ref, w1_ref, w2_ref, o_ref):
    mod = mod_ref[...]
    hn = _mod_norm(x, g_ref[...], mod[0:1], mod[1:2]).astype(BF16)
    f = w2_ref.shape[0]
    a = _dot(hn, w1_ref[:, :f])
    b = _dot(hn, w1_ref[:, f:])
    y = _dot((_silu(a) * b).astype(BF16), w2_ref[...])
    o_ref[...] = x + (FFN_RESIDUAL * mod[2:3]) * y


def _ffn_kernel(x_ref, mod_ref, g_ref, w1_ref, w2_ref, o_ref):
    _ffn_rows(x_ref[...], mod_ref, g_ref, w1_ref, w2_ref, o_ref)


def _ffn_joining_kernel(ctx_ref, x_ref, mod_ref, g_ref, w1_ref, w2_ref, o_ref, *, n_ctx_tiles):
    tile = pl.program_id(1) + jnp.zeros(x_ref.shape, jnp.int32)
    _ffn_rows(jnp.where(tile < n_ctx_tiles, ctx_ref[...], x_ref[...]), mod_ref, g_ref, w1_ref, w2_ref, o_ref)


def _ffn_after_proj_kernel(x_ref, y_ref, wo_ref, mix_mod_ref, mod_ref, g_ref, w1_ref, w2_ref, o_ref):
    x = x_ref[...] + mix_mod_ref[2:3, :] * _dot(y_ref[...].astype(BF16), wo_ref[...])
    _ffn_rows(x, mod_ref, g_ref, w1_ref, w2_ref, o_ref)


def _ffn_after_retention_kernel(x_ref, of_ref, ob_ref, gf_ref, gb_ref, gain_ref, wo_ref, mix_mod_ref,
                                mod_ref, g_ref, w1_ref, w2_ref, o_ref, *, n_heads):
    y = _retention_finish(of_ref, ob_ref, gf_ref, gb_ref, gain_ref, n_heads)
    x = x_ref[...] + mix_mod_ref[2:3, :] * _dot(y, wo_ref[...])
    _ffn_rows(x, mod_ref, g_ref, w1_ref, w2_ref, o_ref)


def _ffn(h, mod, gain, w1, w2, lc, latent_only=False, pending=None):
    n_batch, t, d = h.shape
    f = w2.shape[0]
    tm = _row_tile(lc, t - lc)
    skip = lc // tm if latent_only else 0
    n_tiles = t // tm - skip

    def rows(width, col=0):
        return pl.BlockSpec((None, tm, width), lambda b, i: (b, i + skip, col))

    def whole(shape):
        return _resident(shape, lambda b, i: (0, 0))

    mod_spec = pl.BlockSpec((None, 3, d), _mod_index(n_batch, lc // tm - skip))
    if pending is None:
        body, specs, args = _ffn_kernel, [rows(d)], [h]
    elif pending[0] == "proj":
        _, y, w_out, mix_mod = pending
        k = y.shape[-1]
        body = _ffn_after_proj_kernel
        specs = [rows(d), rows(k), whole((k, d)), mod_spec]
        args = [h, y, w_out, mix_mod]
    else:
        _, o_f, o_b, proj, gn_gain, w_out, mix_mod = pending
        vw = o_f.shape[-1]
        gate_f = proj.shape[-1] // vw - 2
        body = functools.partial(_ffn_after_retention_kernel, n_heads=RET_HEADS)
        specs = [rows(d), rows(vw), rows(vw), rows(vw, gate_f), rows(vw, gate_f + 1),
                 pl.BlockSpec((2, vw), lambda b, i: (0, 0)), whole((vw, d)), mod_spec]
        args = [h, o_f, o_b, proj, proj, gn_gain, w_out, mix_mod]
    return pl.pallas_call(
        body,
        grid=(n_batch, n_tiles),
        in_specs=specs + [mod_spec, pl.BlockSpec((1, d), lambda b, i: (0, 0)), whole((d, 2 * f)), whole((f, d))],
        out_specs=pl.BlockSpec((None, tm, d), lambda b, i: (b, i, 0)),
        out_shape=jax.ShapeDtypeStruct((n_batch, n_tiles * tm, d), F32),
        compiler_params=_params("parallel", "parallel"),
        name="ffn",
    )(*args, mod, gain.reshape(1, d), w1, w2)


def _ffn_joining(ctx, x, mod, gain, w1, w2):
    n_batch, s, d = x.shape
    lc = ctx.shape[1]
    f = w2.shape[0]
    tm = _row_tile(lc, s)
    nct = lc // tm
    return pl.pallas_call(
        functools.partial(_ffn_joining_kernel, n_ctx_tiles=nct),
        grid=(n_batch, (lc + s) // tm),
        in_specs=[
            pl.BlockSpec((None, tm, d), lambda b, i: (b, jnp.minimum(i, nct - 1), 0)),
            pl.BlockSpec((None, tm, d), lambda b, i: (b, jnp.maximum(i - nct, 0), 0)),
            pl.BlockSpec((None, 3, d), _mod_index(n_batch, nct)),
            pl.BlockSpec((1, d), lambda b, i: (0, 0)),
            _resident((d, 2 * f), lambda b, i: (0, 0)),
            _resident((f, d), lambda b, i: (0, 0)),
        ],
        out_specs=pl.BlockSpec((None, tm, d), lambda b, i: (b, i, 0)),
        out_shape=jax.ShapeDtypeStruct((n_batch, lc + s, d), F32),
        compiler_params=_params("parallel", "parallel"),
        name="ffn_joining",
    )(ctx, x, mod, gain.reshape(1, d), w1, w2)


def _proj_kernel(x_ref, mod_ref, g_ref, w_ref, o_ref):
    mod = mod_ref[...]
    hn = _mod_norm(x_ref[...], g_ref[...], mod[0:1], mod[1:2]).astype(BF16)
    o_ref[...] = _dot(hn, w_ref[...]).astype(o_ref.dtype)


PROJ_WEIGHT_TILE_BYTES = 16 * 1024 * 1024


def _col_tile(w):
    d, n = w.shape
    if w.size * w.dtype.itemsize <= PROJ_WEIGHT_TILE_BYTES:
        return n
    tn = 2048
    while n % tn:
        tn -= LANE
    return tn


def _proj(h, mod, gain, w, lc, out_dtype=F32):
    n_batch, t, d = h.shape
    n = w.shape[1]
    tm = _row_tile(lc, t - lc)
    tn = _col_tile(w)
    mod_idx = _mod_index(n_batch, lc // tm)
    w_spec = (_resident if tn == n else pl.BlockSpec)((d, tn), lambda j, b, i: (0, j))
    return pl.pallas_call(
        _proj_kernel,
        grid=(n // tn, n_batch, t // tm),
        in_specs=[
            pl.BlockSpec((None, tm, d), lambda j, b, i: (b, i, 0)),
            pl.BlockSpec((None, 3, d), lambda j, b, i: mod_idx(b, i)),
            pl.BlockSpec((1, d), lambda j, b, i: (0, 0)),
            w_spec,
        ],
        out_specs=pl.BlockSpec((None, tm, tn), lambda j, b, i: (b, i, j)),
        out_shape=jax.ShapeDtypeStruct((n_batch, t, n), out_dtype),
        compiler_params=_params("parallel", "parallel", "parallel"),
        name="mixer_in",
    )(h, mod, gain.reshape(1, d), w)


def _proj_split_kernel(x_ref, mod_ref, g_ref, w_ref, *o_refs):
    mod = mod_ref[...]
    hn = _mod_norm(x_ref[...], g_ref[...], mod[0:1], mod[1:2]).astype(BF16)
    y = _dot(hn, w_ref[...])
    col = 0
    for o_ref in o_refs:
        o_ref[...] = y[:, col:col + o_ref.shape[1]].astype(o_ref.dtype)
        col += o_ref.shape[1]


def _proj_split(h, mod, gain, w, lc, widths, dtypes=None):
    dtypes = dtypes or (F32,) * len(widths)
    n_batch, t, d = h.shape
    n = w.shape[1]
    tm = _row_tile(lc, t - lc)
    return pl.pallas_call(
        _proj_split_kernel,
        grid=(n_batch, t // tm),
        in_specs=[
            pl.BlockSpec((None, tm, d), lambda b, i: (b, i, 0)),
            pl.BlockSpec((None, 3, d), _mod_index(n_batch, lc // tm)),
            pl.BlockSpec((1, d), lambda b, i: (0, 0)),
            _resident((d, n), lambda b, i: (0, 0)),
        ],
        out_specs=[pl.BlockSpec((None, tm, wd), lambda b, i: (b, i, 0)) for wd in widths],
        out_shape=[jax.ShapeDtypeStruct((n_batch, t, wd), dt) for wd, dt in zip(widths, dtypes)],
        compiler_params=_params("parallel", "parallel"),
        name="mixer_in_split",
    )(h, mod, gain.reshape(1, d), w)


def _bmm(a, b):
    return jnp.einsum("gik,gkj->gij", a.astype(BF16), b.astype(BF16), preferred_element_type=F32)


def _pair_mm(x, y, left):
    y = y.astype(BF16)
    zero = jnp.zeros_like(y)
    y_diag = jnp.concatenate([jnp.where(left, y, zero), jnp.where(left, zero, y)], axis=1)
    return jnp.einsum("gik,gkj->gij", x.astype(BF16), y_diag, preferred_element_type=F32)


def _unit_tri_inverse(a, blk_mask, eye_f, left):
    c = a.shape[1]
    d = jnp.where(blk_mask, a, 0.0)
    e = a - d
    p = eye_f - d
    dk = d
    for _ in range(DN_INV_BLOCK.bit_length() - 2):
        dk = _pair_mm(dk, dk, left)
        p = _pair_mm(p, eye_f + dk, left)
    n = _pair_mm(p, e, left)
    x = eye_f - n
    nk = n
    for _ in range((c // DN_INV_BLOCK).bit_length() - 2):
        nk = _pair_mm(nk, nk, left)
        x = _pair_mm(x, eye_f + nk, left)
    return _pair_mm(x, p, left)


def _dn_kernel(q_ref, k_ref, v_ref, z_ref, cwq_ref, cwk_ref, cwv_ref, gcol_ref, grow_ref, og_ref,
               y_ref,
               pad_s, q_s, k_s, v_s, kq_s, b_s, a_s, o_s, *, lc):
    t = q_ref.shape[0]
    c = DN_CHUNK
    nc = t // c
    ncc = lc // c
    rb = _row_tile(lc, t - lc)
    hd = DN_HEAD_DIM

    zeros8 = jnp.zeros((SUBLANE, DN_HEAD_DIM), F32)

    def conv_act(x_ref, cw_ref, dst, normalise):
        pad_s[0:SUBLANE, :] = zeros8
        pad_s[SUBLANE:SUBLANE + lc, :] = x_ref[0:lc, :].astype(F32)
        pad_s[SUBLANE + lc:2 * SUBLANE + lc, :] = zeros8
        pad_s[2 * SUBLANE + lc:2 * SUBLANE + t, :] = x_ref[lc:t, :].astype(F32)
        pad_s[2 * SUBLANE + t:3 * SUBLANE + t, :] = zeros8
        cw = cw_ref[...]
        n_tap = cw.shape[0]
        for r0 in range(0, t, rb):
            base = r0 + (SUBLANE if r0 < lc else 2 * SUBLANE)
            acc = None
            for j in range(n_tap):
                lo = base + j - n_tap // 2
                term = pad_s[lo:lo + rb, :] * cw[j:j + 1, :]
                acc = term if acc is None else acc + term
            act = _silu(acc)
            if normalise:
                act = act * lax.rsqrt(jnp.sum(act * act, axis=-1, keepdims=True) + EPS)
            dst[r0:r0 + rb, :] = act

    conv_act(q_ref, cwq_ref, q_s, True)
    conv_act(k_ref, cwk_ref, k_s, True)
    conv_act(v_ref, cwv_ref, v_s, False)

    ri = lax.broadcasted_iota(jnp.int32, (c, 2 * c), 0)
    li = lax.broadcasted_iota(jnp.int32, (c, 2 * c), 1)
    left = li < c
    ci = jnp.where(left, li, li - c)
    eye_f = jnp.where(ri == ci, 1.0, 0.0).astype(F32)
    blk_mask = (ri // DN_INV_BLOCK) == (ci // DN_INV_BLOCK)
    lag = jnp.where(left, ri - ci, ci - ri)
    incl = lag >= 0
    strict = lag > 0
    q_scale = DN_HEAD_DIM ** -0.5
    g = max(n for n in DN_CHUNK_BATCHES if nc % n == 0)

    def phase1(it, carry):
        ch0 = it * g
        rows = pl.ds(pl.multiple_of(ch0 * c, c), g * c)
        q = q_s[rows, :].reshape(g, c, DN_HEAD_DIM) * q_scale
        k = k_s[rows, :].reshape(g, c, DN_HEAD_DIM)
        v = v_s[rows, :].reshape(g, c, DN_HEAD_DIM)
        gcol = gcol_ref[rows, :].reshape(g, c, 4)
        grow = grow_ref[pl.ds(ch0, g)]
        k_bf = k.astype(BF16)
        gcs = (gcol[:, :, 0:1], gcol[:, :, 2:3])
        kbs = (k * gcol[:, :, 1:2], k * gcol[:, :, 3:4])
        decay = jnp.exp(jnp.where(incl, jnp.where(left, gcs[0], gcs[1]) - grow, NEG_BIG))
        kq = jnp.einsum("gic,gjc->gij", jnp.concatenate([kbs[0], kbs[1], q], axis=1).astype(BF16),
                        jnp.concatenate([k_bf, k_bf], axis=1), preferred_element_type=F32)
        a = jnp.where(strict, jnp.where(left, kq[:, :c], kq[:, c:2 * c]) * decay, 0.0)
        qk = (kq[:, 2 * c:] * decay).astype(BF16)
        tinv = _unit_tri_inverse(a, blk_mask, eye_f, left).astype(BF16)
        for d in range(2):
            gc, kb, beta = gcs[d], kbs[d], gcol[:, :, 2 * d + 1:2 * d + 2]

            def own_rows(x):
                zero = jnp.zeros_like(x)
                return jnp.concatenate([x, zero] if d == 0 else [zero, x], axis=1)

            eg = jnp.exp(gc)
            wu = jnp.einsum("gik,gkj->gij", tinv, own_rows(jnp.concatenate([kb * eg, v * beta], axis=2).astype(BF16)),
                            preferred_element_type=F32).astype(BF16)
            g_tot = gc[:, c - 1:c, :] if d == 0 else gc[:, 0:1, :]
            kd = (k * jnp.exp(g_tot - gc)).astype(BF16)
            kwu = jnp.einsum("gck,gcn->gkn", kd, wu, preferred_element_type=F32)
            qwu = jnp.einsum("gik,gkj->gij", qk, own_rows(wu), preferred_element_type=F32)
            qp = q * eg - qwu[:, :, :hd]
            ob = qwu[:, :, hd:]
            dec = jnp.exp(g_tot)
            first, second = (0, 1) if d == 0 else (1, 0)

            def of_pair(x, which):
                return x.reshape((g // 2, 2) + x.shape[1:])[:, which]

            kb1, kb2 = of_pair(kwu, first), of_pair(kwu, second)
            q1, q2 = of_pair(qp, first), of_pair(qp, second)
            a1, a2 = of_pair(dec, first), of_pair(dec, second)
            cross = _bmm(jnp.concatenate([kb2[:, :, :hd], q2], axis=1), kb1)
            k_pair = a2 * kb1[:, :, :hd] + a1 * kb2[:, :, :hd] - cross[:, :hd, :hd]
            b_pair = a2 * kb1[:, :, hd:] + kb2[:, :, hd:] - cross[:, :hd, hd:]
            q2 = a1 * q2 - cross[:, hd:, :hd]
            ob1, ob2 = of_pair(ob, first), of_pair(ob, second) + cross[:, hd:, hd:]
            q_rows, ob_rows = ((q1, q2), (ob1, ob2)) if d == 0 else ((q2, q1), (ob2, ob1))
            pairs = pl.ds(it * (g // 2), g // 2)
            kq_s[d, pairs, 0:hd, :] = k_pair.astype(BF16)
            kq_s[d, pairs, hd:hd + c, :] = q_rows[0].astype(BF16)
            kq_s[d, pairs, hd + c:, :] = q_rows[1].astype(BF16)
            b_s[d, pairs] = b_pair
            o_s[d, rows, :] = jnp.concatenate(ob_rows, axis=1).reshape(g * c, hd)
            a_s[d, pairs] = jnp.broadcast_to(a1 * a2, (g // 2, SUBLANE, hd))
        return carry

    lax.fori_loop(0, nc // g, phase1, 0)

    n_pairs, n_ctx_pairs = nc // 2, ncc // 2

    def phase2(s, carry):
        pair_of = (s, jnp.where(s < n_ctx_pairs, n_ctx_pairs - 1 - s, n_pairs - 1 - s + n_ctx_pairs))
        new = []
        for d in range(2):
            state = carry[d]
            pr = pair_of[d]
            rows = pl.ds(pl.multiple_of(pr * 2 * c, 2 * c), 2 * c)
            r = _dot(kq_s[d, pr], state.astype(BF16))
            o_s[d, rows, :] = o_s[d, rows, :] + r[hd:]
            new.append(state * a_s[d, pr][0:1, :] + (b_s[d, pr] - r[:hd]))
        return tuple(new)

    zero_state = jnp.zeros((hd, hd), F32)
    lax.fori_loop(0, n_pairs, phase2, (zero_state, zero_state))

    og = og_ref[...]
    for r0 in range(0, t, rb):
        o = o_s[0, r0:r0 + rb, :] + o_s[1, r0:r0 + rb, :]
        o = o * lax.rsqrt(jnp.mean(o * o, axis=-1, keepdims=True) + EPS) * og
        y_ref[r0:r0 + rb, :] = (o * _silu(z_ref[r0:r0 + rb, :].astype(F32))).astype(y_ref.dtype)


def _dn_gates_kernel(ab_ref, alog_ref, dtb_ref, o_ref, *, n_heads):
    t, width = ab_ref.shape
    c = DN_CHUNK
    col = lax.broadcasted_iota(jnp.int32, (1, width), 1)
    is_decay = (col % (2 * n_heads)) < n_heads
    backward = col >= 2 * n_heads
    ri = lax.broadcasted_iota(jnp.int32, (c, c), 0)
    ci = lax.broadcasted_iota(jnp.int32, (c, c), 1)
    tri_f = jnp.where(ci <= ri, 1.0, 0.0).astype(BF16)
    tri_b = jnp.where(ci >= ri, 1.0, 0.0).astype(BF16)
    neg_a = -jnp.exp(alog_ref[...])
    dtb = dtb_ref[...]
    for r0 in range(0, t, c):
        raw = ab_ref[r0:r0 + c, :]
        g = neg_a * _softplus(raw + dtb)
        g1, g2, g3 = _split3(g)
        cum_f = _dot(tri_f, g1) + (_dot(tri_f, g2) + _dot(tri_f, g3))
        cum_b = _dot(tri_b, g1) + (_dot(tri_b, g2) + _dot(tri_b, g3))
        o_ref[r0:r0 + c, :] = jnp.where(is_decay, jnp.where(backward, cum_b, cum_f), _sigmoid(raw))


def _dn_gates(ab, a_log, dt_bias):
    n_batch, t, width = ab.shape
    nh = a_log.shape[-1]

    def param_row(p):
        z = jnp.zeros((nh,), F32)
        row = jnp.concatenate([p[0], z, p[1], z])
        return jnp.pad(row, (0, width - 4 * nh)).reshape(1, width)

    return pl.pallas_call(
        functools.partial(_dn_gates_kernel, n_heads=nh),
        grid=(n_batch,),
        in_specs=[
            pl.BlockSpec((None, t, width), lambda b: (b, 0, 0)),
            pl.BlockSpec((1, width), lambda b: (0, 0)),
            pl.BlockSpec((1, width), lambda b: (0, 0)),
        ],
        out_specs=pl.BlockSpec((None, t, width), lambda b: (b, 0, 0)),
        out_shape=jax.ShapeDtypeStruct(ab.shape, F32),
        compiler_params=_params("parallel"),
        name="deltanet_gates",
    )(ab, param_row(a_log), param_row(dt_bias))


def _dn_core(qkvz, gate_cols, gate_rows, conv_w, o_gain, lc):
    n_batch, t, width = qkvz.shape
    hd = DN_HEAD_DIM
    nh = width // (4 * hd)
    nc = t // DN_CHUNK

    def col_block(offset):
        return pl.BlockSpec((None, t, hd), lambda b, h: (b, 0, offset + h))

    def conv_block(offset):
        return pl.BlockSpec((conv_w.shape[0], hd), lambda b, h: (0, offset + h))

    return pl.pallas_call(
        functools.partial(_dn_kernel, lc=lc),
        grid=(n_batch, nh),
        in_specs=[
            col_block(0), col_block(nh), col_block(2 * nh), col_block(3 * nh),
            conv_block(0), conv_block(nh), conv_block(2 * nh),
            pl.BlockSpec((None, None, t, 4), lambda b, h: (b, h, 0, 0)),
            pl.BlockSpec((None, None, nc, 1, 2 * DN_CHUNK), lambda b, h: (b, h, 0, 0, 0)),
            pl.BlockSpec((1, hd), lambda b, h: (0, 0)),
        ],
        out_specs=pl.BlockSpec((None, t, hd), lambda b, h: (b, 0, h)),
        out_shape=jax.ShapeDtypeStruct((n_batch, t, nh * hd), BF16),
        scratch_shapes=[
            pltpu.VMEM((t + 3 * SUBLANE, hd), F32),
            pltpu.VMEM((t, hd), F32),
            pltpu.VMEM((t, hd), F32),
            pltpu.VMEM((t, hd), F32),
            pltpu.VMEM((2, nc // 2, hd + 2 * DN_CHUNK, hd), BF16),
            pltpu.VMEM((2, nc // 2, hd, hd), F32),
            pltpu.VMEM((2, nc // 2, SUBLANE, hd), F32),
            pltpu.VMEM((2, t, hd), F32),
        ],
        compiler_params=_params("parallel", "parallel"),
        name="deltanet",
    )(qkvz, qkvz, qkvz, qkvz, conv_w, conv_w, conv_w, gate_cols, gate_rows, o_gain.reshape(1, hd))


def _deltanet_mixer(h, mod, gain, w_in, conv_w, a_log, dt_bias, o_gain, w_out, lc):
    n_batch, t, d = h.shape
    nh = a_log.shape[-1]
    width = 4 * nh * DN_HEAD_DIM
    w_pad = jnp.pad(w_in, ((0, 0), (0, LANE - 4 * nh))).astype(BF16)
    qkvz, ab = _proj_split(h, mod, gain, w_pad, lc, (width, LANE), (BF16, F32))
    gates = _dn_gates(ab, a_log, dt_bias)[:, :, :4 * nh]
    gates = gates.reshape(n_batch, t, 2, 2, nh)
    gate_cols = gates.transpose(0, 4, 1, 2, 3).reshape(n_batch, nh, t, 4)
    gate_rows = (gates[:, :, :, 0, :].transpose(0, 3, 2, 1)
                 .reshape(n_batch, nh, 2, t // DN_CHUNK, DN_CHUNK).transpose(0, 1, 3, 2, 4)
                 .reshape(n_batch, nh, t // DN_CHUNK, 1, 2 * DN_CHUNK))
    y = _dn_core(qkvz, gate_cols, gate_rows, conv_w, o_gain, lc)
    return ("proj", y, w_out.astype(BF16), mod)


def _headnorm_rope_kernel(x_ref, gain_ref, cos_ref, sin_ref, e_ref, et_ref, o_ref, *, scale):
    x = x_ref[...]
    width = x.shape[1]
    sq_hi, sq_lo = _split2(x * x)
    ss = _dot(sq_hi, e_ref[...]) + _dot(sq_lo, e_ref[...])
    inv_hi, inv_lo = _split2(lax.rsqrt(ss * (1.0 / SWA_HEAD_DIM) + EPS))
    inv = _dot(inv_hi, et_ref[...]) + _dot(inv_lo, et_ref[...])
    lane = lax.broadcasted_iota(jnp.int32, (1, LANE), 1)
    first_half = (lane % SWA_HEAD_DIM) < SWA_HEAD_DIM // 2
    gain = gain_ref[...]
    cos = cos_ref[...]
    sin = sin_ref[...]
    for s in range(width // LANE):
        xs = x[:, s * LANE:(s + 1) * LANE] * inv[:, s * LANE:(s + 1) * LANE] * gain
        partner = jnp.where(first_half, pltpu.roll(xs, LANE - SWA_HEAD_DIM // 2, 1),
                            pltpu.roll(xs, SWA_HEAD_DIM // 2, 1))
        o_ref[:, s * LANE:(s + 1) * LANE] = (xs * cos + partner * sin) * scale


def _headnorm_rope(x, gain, cos_t, sin_t, scale, lc):
    n_batch, t, width = x.shape
    tm = _row_tile(lc, t - lc)
    heads = lax.broadcasted_iota(jnp.int32, (width, LANE), 0) // SWA_HEAD_DIM
    e = (heads == lax.broadcasted_iota(jnp.int32, (width, LANE), 1)).astype(BF16)
    gain_t = jnp.tile(gain, LANE // SWA_HEAD_DIM).reshape(1, LANE)
    return pl.pallas_call(
        functools.partial(_headnorm_rope_kernel, scale=scale),
        grid=(n_batch, t // tm),
        in_specs=[
            pl.BlockSpec((None, tm, width), lambda b, i: (b, i, 0)),
            pl.BlockSpec((1, LANE), lambda b, i: (0, 0)),
            pl.BlockSpec((tm, LANE), lambda b, i: (i, 0)),
            pl.BlockSpec((tm, LANE), lambda b, i: (i, 0)),
            pl.BlockSpec((width, LANE), lambda b, i: (0, 0)),
            pl.BlockSpec((LANE, width), lambda b, i: (0, 0)),
        ],
        out_specs=pl.BlockSpec((None, tm, width), lambda b, i: (b, i, 0)),
        out_shape=jax.ShapeDtypeStruct(x.shape, F32),
        compiler_params=_params("parallel", "parallel"),
        name="headnorm_rope",
    )(x, gain_t, cos_t, sin_t, e, e.T)


def _attn_kernel(sink_ref, qt_ref, k_ref, vt_ref, o_ref, *, lc):
    n_kv, n_grp, hd, qb = qt_ref.shape
    t = k_ref.shape[1]
    span = qb + 2 * WINDOW
    cols = n_grp * qb
    start = pl.program_id(1) * qb
    is_latent = start >= lc
    ws = pl.multiple_of(jnp.clip(start - WINDOW, lc, t - span), LANE)
    s_pos = ws + lax.broadcasted_iota(jnp.int32, (span, cols), 0)
    t_pos = start + lax.broadcasted_iota(jnp.int32, (span, cols), 1) % qb
    allowed = jnp.abs(t_pos - s_pos) <= jnp.where(is_latent, WINDOW, -1)
    qt = jnp.stack([jnp.concatenate([qt_ref[g, i] for i in range(n_grp)], axis=1) for g in range(n_kv)], axis=0)
    sink = jnp.stack([jnp.concatenate([jnp.full((1, qb), sink_ref[g * n_grp + i], F32) for i in range(n_grp)], axis=1)
                      for g in range(n_kv)], axis=0)
    s_ctx = jnp.einsum("gkd,gdq->gkq", k_ref[:, 0:lc, :], qt, preferred_element_type=F32)
    s_win = jnp.einsum("gkd,gdq->gkq", k_ref[:, pl.ds(ws, span), :], qt, preferred_element_type=F32)
    s_win = jnp.where(allowed, s_win, NEG_BIG)
    m = jnp.maximum(jnp.maximum(jnp.max(s_ctx, axis=1, keepdims=True), jnp.max(s_win, axis=1, keepdims=True)), sink)
    p_ctx = jnp.exp(s_ctx - m)
    p_win = jnp.exp(s_win - m)
    den = jnp.sum(p_ctx, axis=1, keepdims=True) + jnp.sum(p_win, axis=1, keepdims=True) + jnp.exp(sink - m)
    o = (jnp.einsum("gdk,gkq->gdq", vt_ref[:, :, 0:lc], p_ctx.astype(BF16), preferred_element_type=F32)
         + jnp.einsum("gdk,gkq->gdq", vt_ref[:, :, pl.ds(ws, span)], p_win.astype(BF16), preferred_element_type=F32))
    o = (o / den).astype(o_ref.dtype)
    for g in range(n_kv):
        for i in range(n_grp):
            o_ref[g, i] = o[g, :, i * qb:(i + 1) * qb]


def _attention(q, k, v, sink, lc):
    n_batch, t, qw = q.shape
    hd = SWA_HEAD_DIM
    n_kv = k.shape[-1] // hd
    n_grp = qw // (n_kv * hd)
    qt = q.astype(BF16).reshape(n_batch, t, n_kv, n_grp, hd).transpose(0, 2, 3, 4, 1)
    kg = k.astype(BF16).reshape(n_batch, t, n_kv, hd).transpose(0, 2, 1, 3)
    vt = v.astype(BF16).reshape(n_batch, t, n_kv, hd).transpose(0, 2, 3, 1)
    out = pl.pallas_call(
        functools.partial(_attn_kernel, lc=lc),
        grid=(n_batch, t // ATTN_BLOCK),
        in_specs=[
            pl.BlockSpec(memory_space=pltpu.SMEM),
            pl.BlockSpec((None, n_kv, n_grp, hd, ATTN_BLOCK), lambda b, j: (b, 0, 0, 0, j)),
            pl.BlockSpec((None, n_kv, t, hd), lambda b, j: (b, 0, 0, 0)),
            pl.BlockSpec((None, n_kv, hd, t), lambda b, j: (b, 0, 0, 0)),
        ],
        out_specs=pl.BlockSpec((None, n_kv, n_grp, hd, ATTN_BLOCK), lambda b, j: (b, 0, 0, 0, j)),
        out_shape=jax.ShapeDtypeStruct((n_batch, n_kv, n_grp, hd, t), BF16),
        compiler_params=_params("parallel", "parallel"),
        name="window_attention",
    )(sink, qt, kg, vt)
    return out.transpose(0, 4, 1, 2, 3).reshape(n_batch, t, qw)


def _swa_tables(lc, s):
    rows = jnp.repeat(jnp.arange(s // GRID_W), GRID_W).astype(F32)
    cols = jnp.tile(jnp.arange(GRID_W), s // GRID_W).astype(F32)
    n_freq = SWA_HEAD_DIM // 4
    inv = ROPE_BASE ** (-jnp.arange(n_freq, dtype=F32) / n_freq)
    ang = jnp.concatenate([rows[:, None] * inv[None, :], cols[:, None] * inv[None, :]], axis=-1)
    cos, sin = jnp.cos(ang), jnp.sin(ang)
    reps = LANE // SWA_HEAD_DIM
    cos_t = jnp.tile(jnp.concatenate([cos, cos], axis=-1), (1, reps))
    sin_t = jnp.tile(jnp.concatenate([-sin, sin], axis=-1), (1, reps))
    cos_t = jnp.concatenate([jnp.ones((lc, LANE), F32), cos_t], axis=0)
    sin_t = jnp.concatenate([jnp.zeros((lc, LANE), F32), sin_t], axis=0)
    return cos_t, sin_t


def _window_attention_mixer(h, mod, gain, w_qkv, q_gain, k_gain, sink, w_out, lc):
    n_batch, t, d = h.shape
    n_heads = sink.shape[0]
    q_w = n_heads * SWA_HEAD_DIM
    kv_w = (w_qkv.shape[1] - q_w) // 2
    cos_t, sin_t = _swa_tables(lc, t - lc)
    q, k, v = _proj_split(h, mod, gain, w_qkv.astype(BF16), lc, (q_w, kv_w, kv_w))
    q = _headnorm_rope(q, q_gain, cos_t, sin_t, SWA_HEAD_DIM ** -0.5, lc)
    k = _headnorm_rope(k, k_gain, cos_t, sin_t, 1.0, lc)
    y = _attention(q, k, v, sink, lc)
    return ("proj", y, w_out.astype(BF16), mod)


def _rope_halves(x, cos, sin):
    half = x.shape[1] // 2
    x1, x2 = x[:, :half], x[:, half:]
    return jnp.concatenate([x1 * cos - x2 * sin, x1 * sin + x2 * cos], axis=1)


def _ret_kernel(qf_ref, kf_ref, vf_ref, cosf_ref, sinf_ref, qb_ref, kb_ref, vb_ref, cosb_ref, sinb_ref,
                of_ref, ob_ref, sf_ref, sb_ref, *, n_heads):
    c = RET_CHUNK
    rows = qf_ref.shape[0]
    dk = qf_ref.shape[1] // n_heads
    dv = vf_ref.shape[1] // n_heads

    @pl.when(pl.program_id(1) == 0)
    def _():
        sf_ref[...] = jnp.zeros_like(sf_ref)
        sb_ref[...] = jnp.zeros_like(sb_ref)

    ri = lax.broadcasted_iota(jnp.int32, (c, c), 0)
    ci = lax.broadcasted_iota(jnp.int32, (c, c), 1)
    pos = lax.broadcasted_iota(jnp.int32, (c, 1), 0).astype(F32)

    def run(q_ref, k_ref, v_ref, cos_ref, sin_ref, o_ref, s_ref, reverse):
        rel = ((ci - ri) if reverse else (ri - ci))
        p = (c - 1.0 - pos) if reverse else pos
        n_sub = rows // c
        order = range(n_sub - 1, -1, -1) if reverse else range(n_sub)
        for hh in range(n_heads):
            lg = math.log1p(-2.0 ** (-5 - hh))
            decay = jnp.exp(jnp.where(rel >= 0, lg * rel.astype(F32), NEG_BIG))
            q_fac = jnp.exp(lg * (p + 1.0))
            k_fac = jnp.exp(lg * (c - 1.0 - p))
            chunk_decay = math.exp(lg * c)
            qc = slice(hh * dk, (hh + 1) * dk)
            vc = slice(hh * dv, (hh + 1) * dv)
            state = s_ref[hh]
            for i in order:
                sl = slice(i * c, (i + 1) * c)
                q = _rope_halves(q_ref[sl, qc].astype(F32), cos_ref[sl, :], sin_ref[sl, :])
                k = _rope_halves(k_ref[sl, qc].astype(F32), cos_ref[sl, :], sin_ref[sl, :]) * (dk ** -0.5)
                v = v_ref[sl, vc].astype(BF16)
                scores = _dot_nt(q.astype(BF16), k.astype(BF16)) * decay
                o = _dot(scores.astype(BF16), v) + _dot((q * q_fac).astype(BF16), state.astype(BF16))
                o_ref[sl, vc] = o.astype(o_ref.dtype)
                state = state * chunk_decay + _dot_tn((k * k_fac).astype(BF16), v)
            s_ref[hh] = state

    run(qf_ref, kf_ref, vf_ref, cosf_ref, sinf_ref, of_ref, sf_ref, False)
    run(qb_ref, kb_ref, vb_ref, cosb_ref, sinb_ref, ob_ref, sb_ref, True)


def _retention_core(proj, cos_t, sin_t, lc):
    n_batch, t, width = proj.shape
    nh = RET_HEADS
    dk = width // (8 * nh)
    dv = 2 * dk
    rows = lc
    n_steps = t // rows
    n_ctx = lc // rows

    def fwd(b, s):
        return s

    def bwd(b, s):
        return jnp.where(s < n_ctx, n_ctx - 1 - s, n_steps - 1 - s + n_ctx)

    def specs(step):
        return [
            pl.BlockSpec((None, rows, nh * dk), lambda b, s: (b, step(b, s), 0)),
            pl.BlockSpec((None, rows, nh * dk), lambda b, s: (b, step(b, s), 1)),
            pl.BlockSpec((None, rows, nh * dv), lambda b, s: (b, step(b, s), 1)),
            pl.BlockSpec((rows, dk // 2), lambda b, s: (step(b, s), 0)),
            pl.BlockSpec((rows, dk // 2), lambda b, s: (step(b, s), 0)),
        ]

    out_sds = jax.ShapeDtypeStruct((n_batch, t, nh * dv), BF16)
    return pl.pallas_call(
        functools.partial(_ret_kernel, n_heads=nh),
        grid=(n_batch, n_steps),
        in_specs=specs(fwd) + specs(bwd),
        out_specs=[
            pl.BlockSpec((None, rows, nh * dv), lambda b, s: (b, fwd(b, s), 0)),
            pl.BlockSpec((None, rows, nh * dv), lambda b, s: (b, bwd(b, s), 0)),
        ],
        out_shape=[out_sds, out_sds],
        scratch_shapes=[pltpu.VMEM((nh, dk, dv), F32), pltpu.VMEM((nh, dk, dv), F32)],
        compiler_params=_params("parallel", "arbitrary"),
        name="retention",
    )(proj, proj, proj, cos_t, sin_t, proj, proj, proj, cos_t, sin_t)


def _retention_finish(of_ref, ob_ref, gf_ref, gb_ref, gain_ref, n_heads):
    dv = of_ref.shape[1] // n_heads
    parts = []
    for hh in range(n_heads):
        sl = slice(hh * dv, (hh + 1) * dv)

        def group_norm(o, gain):
            mu = jnp.mean(o, axis=-1, keepdims=True)
            xc = o - mu
            var = jnp.mean(xc * xc, axis=-1, keepdims=True)
            return xc * lax.rsqrt(var + EPS) * gain

        y = (_silu(gf_ref[:, sl].astype(F32)) * group_norm(of_ref[:, sl].astype(F32), gain_ref[0:1, sl])
             + _silu(gb_ref[:, sl].astype(F32)) * group_norm(ob_ref[:, sl].astype(F32), gain_ref[1:2, sl]))
        parts.append(y.astype(BF16))
    return jnp.concatenate(parts, axis=1)


def _ret_tables(lc, s, n_freq):
    inv = ROPE_BASE ** (-jnp.arange(n_freq, dtype=F32) / n_freq)
    ang = jnp.arange(s, dtype=F32)[:, None] * inv[None, :]
    cos_t = jnp.concatenate([jnp.ones((lc, n_freq), F32), jnp.cos(ang)], axis=0)
    sin_t = jnp.concatenate([jnp.zeros((lc, n_freq), F32), jnp.sin(ang)], axis=0)
    return cos_t, sin_t


def _retention_mixer(h, mod, gain, w_in, gn_gain, w_out, lc):
    n_batch, t, d = h.shape
    dk = w_in.shape[1] // (8 * RET_HEADS)
    cos_t, sin_t = _ret_tables(lc, t - lc, dk // 2)
    proj = _proj(h, mod, gain, w_in.astype(BF16), lc, out_dtype=BF16)
    o_f, o_b = _retention_core(proj, cos_t, sin_t, lc)
    return ("retention", o_f, o_b, proj, gn_gain, w_out.astype(BF16), mod)


def kernel(x, c, ctx, c_ctx, ada_w, ada_b, norm_g, ffn_w1, ffn_w2, dn_w_in, dn_conv, dn_a_log, dn_dt_bias,
           dn_o_gain, dn_w_out, swa_w_qkv, swa_q_gain, swa_k_gain, swa_sink, swa_w_out, ret_w_in, ret_gn_gain,
           ret_w_out):
    n_batch, s, d = x.shape
    lc = ctx.shape[1]
    depth = ada_w.shape[0]
    n_mod = ada_w.shape[2] // d

    bp = -(-(n_batch + 1) // SUBLANE) * SUBLANE
    cond = jnp.concatenate([c, c_ctx[None, :], jnp.zeros((bp - n_batch - 1, d), F32)], axis=0)
    mods = _ada_mods(cond, ada_w, ada_b).reshape(depth, bp, n_mod, d)

    w1 = ffn_w1.astype(BF16)
    w2 = ffn_w2.astype(BF16)
    h = None
    for i in range(depth):
        kind, slot = i % N_MIXERS, i // N_MIXERS
        sub = [mods[i, :, 3 * j:3 * j + 3, :] for j in range(3)]
        if i == 0:
            h = _ffn_joining(ctx, x, sub[0], norm_g[i, 0], w1[i, 0], w2[i, 0])
        else:
            h = _ffn(h, sub[0], norm_g[i, 0], w1[i, 0], w2[i, 0], lc)
        if kind == 0:
            mixed = _deltanet_mixer(h, sub[1], norm_g[i, 1], dn_w_in[slot], dn_conv[slot], dn_a_log[slot],
                                    dn_dt_bias[slot], dn_o_gain[slot], dn_w_out[slot], lc)
        elif kind == 1:
            mixed = _window_attention_mixer(h, sub[1], norm_g[i, 1], swa_w_qkv[slot], swa_q_gain[slot],
                                            swa_k_gain[slot], swa_sink[slot], swa_w_out[slot], lc)
        else:
            mixed = _retention_mixer(h, sub[1], norm_g[i, 1], ret_w_in[slot], ret_gn_gain[slot], ret_w_out[slot],
                                     lc)
        h = _ffn(h, sub[2], norm_g[i, 2], w1[i, 1], w2[i, 1], lc, latent_only=(i == depth - 1), pending=mixed)
    return h
```
